```python
import math
import jax, jax.numpy as jnp
from jax import lax
import numpy as np

D_MODEL = 1024
BATCH = 2
SEQ = 8192
DEPTH = 2

N_META = 16
N_MIXERS = 2
N_LRU_LAYERS = (DEPTH + 1) // 2
N_ATTN_LAYERS = DEPTH // 2

LRU_WIDTH = D_MODEL
LRU_HEADS = 8
LRU_BLOCK = LRU_WIDTH // LRU_HEADS
LRU_CONV = 4
LRU_C = 8.0

ATTN_HEAD_DIM = 64
ATTN_HEADS = D_MODEL // (2 * ATTN_HEAD_DIM)
ROPE_THETA = 10000.0
Q_BLOCK = 128

FFN_DIM = 3 * D_MODEL
FFN_CONV = 3

NORM_EPS = 1e-6

kernel_name = "hybrid_rglru_diffattn_convffn"


def rmsnorm(x, g):
    xf = x.astype(jnp.float32)
    y = xf * lax.rsqrt(jnp.mean(xf * xf, axis=-1, keepdims=True) + NORM_EPS)
    return (y * g.astype(jnp.float32)).astype(x.dtype)


def causal_depthwise_conv(x, w, b):
    K, C = w.shape
    out = lax.conv_general_dilated(
        x, w[:, None, :].astype(x.dtype), window_strides=(1,), padding=[(K - 1, 0)],
        dimension_numbers=("NWC", "WIO", "NWC"), feature_group_count=C)
    return out + b.astype(x.dtype)


def rg_lru(x, w_a, b_a, w_i, b_i, lam):
    B, T, W = x.shape
    xb = x.reshape(B, T, LRU_HEADS, LRU_BLOCK)
    r = jax.nn.sigmoid(jnp.einsum('bthi,hij->bthj', xb, w_a).reshape(B, T, W) + b_a)
    gi = jax.nn.sigmoid(jnp.einsum('bthi,hij->bthj', xb, w_i).reshape(B, T, W) + b_i)
    log_a = LRU_C * r.astype(jnp.float32) * jax.nn.log_sigmoid(lam.astype(jnp.float32))
    a = jnp.exp(log_a)
    mult = jnp.sqrt(-jnp.expm1(2.0 * log_a))
    u = mult * (gi.astype(jnp.float32) * x.astype(jnp.float32))

    def combine(left, right):
        a_l, h_l = left
        a_r, h_r = right
        return a_l * a_r, a_r * h_l + h_r

    _, h = lax.associative_scan(combine, (a, u), axis=1)
    return h.astype(x.dtype)


def recurrent_block(x, w_in, b_in, conv_w, conv_b, w_a, b_a, w_i, b_i, lam, w_out, b_out):
    u = x @ w_in + b_in
    gate, rec = jnp.split(u, 2, axis=-1)
    rec = causal_depthwise_conv(rec, conv_w, conv_b)
    h = rg_lru(rec, w_a, b_a, w_i, b_i, lam)
    y = jax.nn.gelu(gate, approximate=True) * h
    return y @ w_out + b_out


def rope_tables(T, dtype):
    inv = 1.0 / (ROPE_THETA ** (jnp.arange(0, ATTN_HEAD_DIM, 2, dtype=jnp.float32) / ATTN_HEAD_DIM))
    ang = jnp.arange(T, dtype=jnp.float32)[:, None] * inv[None, :]
    ang = jnp.concatenate([ang, ang], axis=-1)
    return jnp.cos(ang).astype(dtype), jnp.sin(ang).astype(dtype)


def apply_rope(x, cos, sin):
    half = ATTN_HEAD_DIM // 2
    x1, x2 = x[..., :half], x[..., half:]
    return x * cos + jnp.concatenate([-x2, x1], axis=-1) * sin


def diff_attention(x, w_qkv, lq1, lk1, lq2, lk2, subln_g, w_o, lambda_init):
    B, T, _ = x.shape
    H, d = ATTN_HEADS, ATTN_HEAD_DIM
    q, k, v = jnp.split(x @ w_qkv, 3, axis=-1)
    q = q.reshape(B, T, H, 2, d).transpose(0, 2, 3, 1, 4)
    k = k.reshape(B, T, H, 2, d).transpose(0, 2, 3, 1, 4)
    v = v.reshape(B, T, H, 2 * d).transpose(0, 2, 1, 3)
    cos, sin = rope_tables(T, x.dtype)
    q = apply_rope(q, cos, sin) * (d ** -0.5)
    k = apply_rope(k, cos, sin)
    lam = (jnp.exp(jnp.sum(lq1.astype(jnp.float32) * lk1.astype(jnp.float32)))
           - jnp.exp(jnp.sum(lq2.astype(jnp.float32) * lk2.astype(jnp.float32)))
           + lambda_init)
    k_pos = jnp.arange(T, dtype=jnp.int32)

    def attend(q_blk, q_pos):
        s = jnp.einsum('bhmqd,bhmkd->bhmqk', q_blk, k).astype(jnp.float32)
        mask = k_pos[None, :] <= q_pos[:, None]
        s = jnp.where(mask, s, -jnp.inf)
        p = jax.nn.softmax(s, axis=-1)
        wgt = p[:, :, 0] - lam * p[:, :, 1]
        return jnp.einsum('bhqk,bhkv->bhqv', wgt.astype(v.dtype), v)

    o_meta = attend(q[:, :, :, :N_META], k_pos[:N_META])
    n_blk = (T - N_META) // Q_BLOCK
    q_real = q[:, :, :, N_META:].reshape(B, H, 2, n_blk, Q_BLOCK, d)
    q_real = jnp.moveaxis(q_real, 3, 0)
    pos_real = k_pos[N_META:].reshape(n_blk, Q_BLOCK)
    o_real = lax.map(lambda args: attend(args[0], args[1]), (q_real, pos_real))
    o_real = jnp.moveaxis(o_real, 0, 2).reshape(B, H, n_blk * Q_BLOCK, 2 * d)
    o = jnp.concatenate([o_meta, o_real], axis=2)
    o = rmsnorm(o, subln_g) * (1.0 - lambda_init)
    o = o.transpose(0, 2, 1, 3).reshape(B, T, H * 2 * d)
    return o @ w_o


def conv_ffn(x, w_up, conv_w, conv_b, w_down):
    u = causal_depthwise_conv(x @ w_up, conv_w, conv_b)
    g, val = jnp.split(u, 2, axis=-1)
    return (jax.nn.gelu(g, approximate=True) * val) @ w_down


def setup_inputs(seed: int = 0) -> dict:
    key = jax.random.key(seed)
    ks = jax.random.split(key, 32)
    f32 = jnp.float32
    D, W, F, H, d = D_MODEL, LRU_WIDTH, FFN_DIM, ATTN_HEADS, ATTN_HEAD_DIM
    nl, na = N_LRU_LAYERS, N_ATTN_LAYERS

    def nrm(k, shape, scale):
        return jax.random.normal(k, shape, f32) * scale

    u = jax.random.uniform(ks[10], (nl, W), f32, 0.9, 0.999)
    base = u ** (1.0 / LRU_C)
    lru_L = jnp.log(base) - jnp.log1p(-base)
    return {
        "x": nrm(ks[0], (BATCH, SEQ, D), 1.0),
        "meta_tokens": nrm(ks[1], (N_META, D), 1.0),
        "mix_norm_g": 1.0 + nrm(ks[2], (DEPTH, D), 0.01),
        "lru_w_in": nrm(ks[3], (nl, D, 2 * W), D ** -0.5),
        "lru_b_in": nrm(ks[4], (nl, 2 * W), 0.01),
        "lru_conv_w": nrm(ks[5], (nl, LRU_CONV, W), LRU_CONV ** -0.5),
        "lru_conv_b": nrm(ks[6], (nl, W), 0.01),
        "lru_w_a": nrm(ks[7], (nl, LRU_HEADS, LRU_BLOCK, LRU_BLOCK), LRU_BLOCK ** -0.5),
        "lru_b_a": nrm(ks[8], (nl, W), 0.01),
        "lru_w_i": nrm(ks[9], (nl, LRU_HEADS, LRU_BLOCK, LRU_BLOCK), LRU_BLOCK ** -0.5),
        "lru_b_i": nrm(ks[11], (nl, W), 0.01),
        "lru_L": lru_L,
        "lru_w_out": nrm(ks[12], (nl, W, D), W ** -0.5),
        "lru_b_out": nrm(ks[13], (nl, D), 0.01),
        "attn_w_qkv": nrm(ks[14], (na, D, 3 * H * 2 * d), D ** -0.5),
        "attn_lambda_q1": nrm(ks[15], (na, d), 0.1),
        "attn_lambda_k1": nrm(ks[16], (na, d), 0.1),
        "attn_lambda_q2": nrm(ks[17], (na, d), 0.1),
        "attn_lambda_k2": nrm(ks[18], (na, d), 0.1),
        "attn_subln_g": 1.0 + nrm(ks[19], (na, 2 * d), 0.01),
        "attn_w_o": nrm(ks[20], (na, H * 2 * d, D), (H * 2 * d) ** -0.5),
        "ffn_norm_g": 1.0 + nrm(ks[21], (DEPTH, D), 0.01),
        "ffn_w_up": nrm(ks[22], (DEPTH, D, 2 * F), D ** -0.5),
        "ffn_conv_w": nrm(ks[23], (DEPTH, FFN_CONV, 2 * F), FFN_CONV ** -0.5),
        "ffn_conv_b": nrm(ks[24], (DEPTH, 2 * F), 0.01),
        "ffn_w_down": nrm(ks[25], (DEPTH, F, D), F ** -0.5),
        "final_norm_g": 1.0 + nrm(ks[26], (D,), 0.01),
    }


def reference(x, meta_tokens, mix_norm_g, lru_w_in, lru_b_in, lru_conv_w, lru_conv_b, lru_w_a, lru_b_a,
              lru_w_i, lru_b_i, lru_L, lru_w_out, lru_b_out, attn_w_qkv, attn_lambda_q1, attn_lambda_k1,
              attn_lambda_q2, attn_lambda_k2, attn_subln_g, attn_w_o, ffn_norm_g, ffn_w_up, ffn_conv_w,
              ffn_conv_b, ffn_w_down, final_norm_g):
    B = x.shape[0]
    meta = jnp.broadcast_to(meta_tokens[None].astype(x.dtype), (B, N_META, D_MODEL))
    h = jnp.concatenate([meta, x], axis=1)
    for i in range(DEPTH):
        j = i // N_MIXERS
        hn = rmsnorm(h, mix_norm_g[i])
        if i % N_MIXERS == 0:
            h = h + recurrent_block(hn, lru_w_in[j], lru_b_in[j], lru_conv_w[j], lru_conv_b[j],
                                    lru_w_a[j], lru_b_a[j], lru_w_i[j], lru_b_i[j], lru_L[j],
                                    lru_w_out[j], lru_b_out[j])
        else:
            lambda_init = 0.8 - 0.6 * math.exp(-0.3 * i)
            h = h + diff_attention(hn, attn_w_qkv[j], attn_lambda_q1[j], attn_lambda_k1[j],
                                   attn_lambda_q2[j], attn_lambda_k2[j], attn_subln_g[j], attn_w_o[j],
                                   lambda_init)
        h = h + conv_ffn(rmsnorm(h, ffn_norm_g[i]), ffn_w_up[i], ffn_conv_w[i], ffn_conv_b[i], ffn_w_down[i])
    h = rmsnorm(h, final_norm_g)
    return h[:, N_META:, :]
```

```python
import functools
import math

import jax
import jax.numpy as jnp
from jax import lax
from jax.experimental import pallas as pl
from jax.experimental.pallas import tpu as pltpu

f32 = jnp.float32
bf16 = jnp.bfloat16

D = 1024
FF = 3072
NH = 8
HW = 128
HD = 64
SEQ = 8192
N_META = 16
PAD = 112
TP = PAD + N_META + SEQ
TM = 640
NT = TP // TM
TQ = 640
TK = 640
HALO = 16
FC = 512
EPS = 1e-6
NEG = -1e30
VMEM_LIMIT = 56 * 1024 * 1024


def _rmsnorm(x, g):
    ms = jnp.mean(x * x, axis=-1, keepdims=True)
    return x * lax.rsqrt(ms + EPS) * g


def _gelu_tanh(x):
    c = math.sqrt(2.0 / math.pi)
    return 0.5 * x * (1.0 + jnp.tanh(c * (x + 0.044715 * (x * x * x))))


def _full(shape):
    n = len(shape)
    return pl.BlockSpec(shape, lambda *_: (0,) * n)


def _lru_kernel(h_ref, g_ref, win_ref, bin_ref, cw_ref, cb_ref, wg_ref, ba_ref, bi_ref,
                lam_ref, wout_ref, bout_ref, o_ref, xbuf, a_scr, u_scr, hc):
    c = pl.program_id(1)
    x = h_ref[0]
    hn = _rmsnorm(x, g_ref[...]).astype(bf16)
    u = jnp.dot(hn, win_ref[...], preferred_element_type=f32) + bin_ref[...]
    gate = u[:, :D]
    row = c * TM + lax.broadcasted_iota(jnp.int32, (TM, 1), 0)
    valid = row >= PAD
    rec = jnp.where(valid, u[:, D:], 0.0)

    @pl.when(c == 0)
    def _():
        xbuf[0:8, :] = jnp.zeros((8, D), f32)
        hc[...] = jnp.zeros((8, D), f32)

    xbuf[8:8 + TM, :] = rec
    cw = cw_ref[...]
    conv = (cb_ref[...] + cw[3:4] * rec + cw[2:3] * xbuf[7:7 + TM, :]
            + cw[1:2] * xbuf[6:6 + TM, :] + cw[0:1] * xbuf[5:5 + TM, :])
    xbuf[0:8, :] = xbuf[TM:TM + 8, :]

    cbf = conv.astype(bf16)
    pa, pi = [], []
    for hd in range(NH):
        ga = jnp.dot(cbf[:, hd * HW:(hd + 1) * HW], wg_ref[hd], preferred_element_type=f32)
        pa.append(ga[:, :HW])
        pi.append(ga[:, HW:])
    r = jax.nn.sigmoid(jnp.concatenate(pa, axis=1) + ba_ref[...])
    gi = jax.nn.sigmoid(jnp.concatenate(pi, axis=1) + bi_ref[...])
    lam = lam_ref[...]
    log_sig = jnp.minimum(lam, 0.0) - jnp.log1p(jnp.exp(-jnp.abs(lam)))
    log_a = r * (8.0 * log_sig)
    a = jnp.exp(log_a)
    mult = jnp.sqrt(1.0 - a * a)
    a_scr[...] = a
    u_scr[...] = jnp.where(valid, mult * (gi * conv), 0.0)

    sub = lax.broadcasted_iota(jnp.int32, (8, D), 0)

    def tile(i, carry):
        off = pl.multiple_of(i * 8, 8)
        av = a_scr[pl.ds(off, 8), :]
        hv = u_scr[pl.ds(off, 8), :]
        for s in (1, 2, 4):
            keep = sub >= s
            a_sh = jnp.where(keep, pltpu.roll(av, s, 0), 1.0)
            h_sh = jnp.where(keep, pltpu.roll(hv, s, 0), 0.0)
            hv = av * h_sh + hv
            av = av * a_sh
        hv = hv + av * carry
        u_scr[pl.ds(off, 8), :] = hv
        return jnp.broadcast_to(hv[7:8, :], (8, D))

    hc[...] = lax.fori_loop(0, TM // 8, tile, hc[...], unroll=4)

    y = (_gelu_tanh(gate) * u_scr[...]).astype(bf16)
    o_ref[0] = x + jnp.dot(y, wout_ref[...], preferred_element_type=f32) + bout_ref[...]


def _lru_block(h, g, w_in, b_in, conv_w, conv_b, w_gate, b_a, b_i, lam, w_out, b_out):
    B = h.shape[0]
    tile = pl.BlockSpec((1, TM, D), lambda b, c: (b, c, 0))
    return pl.pallas_call(
        _lru_kernel,
        grid=(B, NT),
        in_specs=[tile, _full((1, D)), _full((D, 2 * D)), _full((1, 2 * D)), _full((4, D)),
                  _full((1, D)), _full((NH, HW, 2 * HW)), _full((1, D)), _full((1, D)),
                  _full((1, D)), _full((D, D)), _full((1, D))],
        out_specs=tile,
        out_shape=jax.ShapeDtypeStruct((B, TP, D), f32),
        scratch_shapes=[pltpu.VMEM((TM + 8, D), f32), pltpu.VMEM((TM, D), f32),
                        pltpu.VMEM((TM, D), f32), pltpu.VMEM((8, D), f32)],
        compiler_params=pltpu.CompilerParams(
            dimension_semantics=("arbitrary", "arbitrary"), vmem_limit_bytes=VMEM_LIMIT),
        name="lru_block",
    )(h, g, w_in, b_in, conv_w, conv_b, w_gate, b_a, b_i, lam, w_out, b_out)


def _qkv_kernel(h_ref, g_ref, w_ref, cos_ref, sin_ref, qt_ref, k_ref, vt_ref):
    hn = _rmsnorm(h_ref[0], g_ref[...]).astype(bf16)
    qkv = jnp.dot(hn, w_ref[...], preferred_element_type=f32)
    cos = cos_ref[...]
    sin = sin_ref[...]
    lane = lax.broadcasted_iota(jnp.int32, (TM, HW), 1)
    first = (lane & (HD - 1)) < HD // 2

    def rope(t):
        rot = jnp.where(first, pltpu.roll(t, HW - HD // 2, 1), pltpu.roll(t, HD // 2, 1))
        return t * cos + rot * sin

    for hd in range(NH):
        qh = rope(qkv[:, hd * HW:(hd + 1) * HW]) * (HD ** -0.5)
        qt_ref[0, hd] = qh.T.astype(bf16)
        kh = rope(qkv[:, D + hd * HW:D + (hd + 1) * HW])
        k_ref[0, :, hd * HW:(hd + 1) * HW] = kh.astype(bf16)
        vh = qkv[:, 2 * D + hd * HW:2 * D + (hd + 1) * HW]
        vt_ref[0, hd, 0] = vh.T.astype(bf16)


def _qkv_proj(h, g, w_qkv, cos, sin):
    B = h.shape[0]
    return pl.pallas_call(
        _qkv_kernel,
        grid=(B, NT),
        in_specs=[pl.BlockSpec((1, TM, D), lambda b, c: (b, c, 0)), _full((1, D)),
                  _full((D, 3 * D)),
                  pl.BlockSpec((TM, HW), lambda b, c: (c, 0)),
                  pl.BlockSpec((TM, HW), lambda b, c: (c, 0))],
        out_specs=[pl.BlockSpec((1, NH, HW, TM), lambda b, c: (b, 0, 0, c)),
                   pl.BlockSpec((1, TM, D), lambda b, c: (b, c, 0)),
                   pl.BlockSpec((1, NH, 1, HW, TM), lambda b, c: (b, 0, c, 0, 0))],
        out_shape=[jax.ShapeDtypeStruct((B, NH, HW, TP), bf16),
                   jax.ShapeDtypeStruct((B, TP, D), bf16),
                   jax.ShapeDtypeStruct((B, NH, NT, HW, TM), bf16)],
        compiler_params=pltpu.CompilerParams(
            dimension_semantics=("arbitrary", "arbitrary"), vmem_limit_bytes=VMEM_LIMIT),
        name="qkv_proj",
    )(h, g, w_qkv, cos, sin)


def _attn_kernel(qt_ref, k_ref, vt_ref, lq1_ref, lk1_ref, lq2_ref, lk2_ref, sg_ref, o_ref,
                 m_scr, l_scr, acc_scr, *, lambda_init):
    i = pl.program_id(2)
    qt = qt_ref[0, 0]
    feat = lax.broadcasted_iota(jnp.int32, (HW, TQ), 0)
    zero = jnp.zeros((HW, TQ), bf16)
    q2 = jnp.concatenate([jnp.where(feat < HD, qt, zero), jnp.where(feat >= HD, qt, zero)], axis=1)

    m_scr[...] = jnp.full((1, 2 * TQ), NEG, f32)
    l_scr[...] = jnp.zeros((1, 2 * TQ), f32)
    acc_scr[...] = jnp.zeros((HW, 2 * TQ), f32)

    def step(j, masked):
        koff = pl.multiple_of(j * TK, TK)
        s = jnp.dot(k_ref[0, pl.ds(koff, TK), :], q2, preferred_element_type=f32)
        if masked:
            kpos = j * TK + lax.broadcasted_iota(jnp.int32, (TK, 2 * TQ), 0)
            col = lax.broadcasted_iota(jnp.int32, (TK, 2 * TQ), 1)
            qpos = i * TQ + jnp.where(col >= TQ, col - TQ, col)
            s = jnp.where((kpos <= qpos) & (kpos >= PAD), s, NEG)
        m_prev = m_scr[...]
        m_new = jnp.maximum(m_prev, jnp.max(s, axis=0, keepdims=True))
        alpha = jnp.exp(m_prev - m_new)
        p = jnp.exp(s - m_new)
        l_scr[...] = alpha * l_scr[...] + jnp.sum(p, axis=0, keepdims=True)
        acc_scr[...] = alpha * acc_scr[...] + jnp.dot(
            vt_ref[0, 0, j], p.astype(bf16), preferred_element_type=f32)
        m_scr[...] = m_new

    step(0, True)

    def body(j, carry):
        step(j, False)
        return carry

    lax.fori_loop(1, i, body, 0)

    @pl.when(i > 0)
    def _():
        step(i, True)

    o = acc_scr[...] * (1.0 / l_scr[...])
    lam = (jnp.exp(jnp.sum(lq1_ref[...] * lk1_ref[...], axis=1, keepdims=True))
           - jnp.exp(jnp.sum(lq2_ref[...] * lk2_ref[...], axis=1, keepdims=True)) + lambda_init)
    ot = (o[:, :TQ] - lam * o[:, TQ:]).T
    y = _rmsnorm(ot, sg_ref[...]) * (1.0 - lambda_init)
    o_ref[0] = y.astype(bf16)


def _diff_attention(qt, k, vt, lq1, lk1, lq2, lk2, subln_g, lambda_init):
    B = k.shape[0]
    return pl.pallas_call(
        functools.partial(_attn_kernel, lambda_init=lambda_init),
        grid=(B, NH, TP // TQ),
        in_specs=[pl.BlockSpec((1, 1, HW, TQ), lambda b, h, i: (b, h, 0, i)),
                  pl.BlockSpec((1, TP, HW), lambda b, h, i: (b, 0, h)),
                  pl.BlockSpec((1, 1, NT, HW, TM), lambda b, h, i: (b, h, 0, 0, 0)),
                  _full((1, HD)), _full((1, HD)), _full((1, HD)), _full((1, HD)),
                  _full((1, HW))],
        out_specs=pl.BlockSpec((1, TQ, HW), lambda b, h, i: (b, i, h)),
        out_shape=jax.ShapeDtypeStruct((B, TP, D), bf16),
        scratch_shapes=[pltpu.VMEM((1, 2 * TQ), f32), pltpu.VMEM((1, 2 * TQ), f32),
                        pltpu.VMEM((HW, 2 * TQ), f32)],
        compiler_params=pltpu.CompilerParams(
            dimension_semantics=("arbitrary", "arbitrary", "arbitrary"),
            vmem_limit_bytes=VMEM_LIMIT),
        name="diff_attention",
    )(qt, k, vt, lq1, lk1, lq2, lk2, subln_g)


def _oproj_kernel(a_ref, w_ref, h_ref, o_ref):
    o_ref[...] = h_ref[...] + jnp.dot(a_ref[...], w_ref[...], preferred_element_type=f32)


def _out_proj(attn, w_o, h):
    rows = h.shape[0]
    tile = pl.BlockSpec((TM, D), lambda r: (r, 0))
    return pl.pallas_call(
        _oproj_kernel,
        grid=(rows // TM,),
        in_specs=[tile, _full((D, D)), tile],
        out_specs=tile,
        out_shape=jax.ShapeDtypeStruct((rows, D), f32),
        compiler_params=pltpu.CompilerParams(
            dimension_semantics=("arbitrary",), vmem_limit_bytes=VMEM_LIMIT),
        name="attn_out_proj",
    )(attn, w_o, h)


def _ffn_kernel(h_ref, halo_ref, g_ref, wup_ref, cw_ref, cb_ref, wdn_ref, fg_ref, o_ref, *, final):
    c = pl.program_id(0) % NT
    xm = h_ref[...]
    xe = jnp.concatenate([halo_ref[...], xm], axis=0)
    t = c * TM - HALO + lax.broadcasted_iota(jnp.int32, (HALO + TM, 1), 0)
    hn = jnp.where(t >= PAD, _rmsnorm(xe, g_ref[...]), 0.0).astype(bf16)

    def conv(u, lo):
        w = cw_ref[:, lo:lo + FC]
        return (cb_ref[:, lo:lo + FC] + w[2:3] * u[HALO:HALO + TM] + w[1:2] * u[HALO - 1:HALO - 1 + TM]
                + w[0:1] * u[HALO - 2:HALO - 2 + TM])

    acc = xm
    for ch in range(FF // FC):
        lo = ch * FC
        ug = jnp.dot(hn, wup_ref[:, lo:lo + FC], preferred_element_type=f32)
        uv = jnp.dot(hn, wup_ref[:, FF + lo:FF + lo + FC], preferred_element_type=f32)
        act = (_gelu_tanh(conv(ug, lo)) * conv(uv, FF + lo)).astype(bf16)
        acc = acc + jnp.dot(act, wdn_ref[lo:lo + FC, :], preferred_element_type=f32)
    if final:
        acc = _rmsnorm(acc, fg_ref[...])
    o_ref[...] = acc


def _conv_ffn(h, g, w_up, conv_w, conv_b, w_down, final_g, final):
    rows = h.shape[0]
    tile = pl.BlockSpec((TM, D), lambda r: (r, 0))
    halo = pl.BlockSpec((HALO, D), lambda r: (jnp.maximum(r * (TM // HALO) - 1, 0), 0))
    return pl.pallas_call(
        functools.partial(_ffn_kernel, final=final),
        grid=(rows // TM,),
        in_specs=[tile, halo, _full((1, D)), _full((D, 2 * FF)), _full((3, 2 * FF)),
                  _full((1, 2 * FF)), _full((FF, D)), _full((1, D))],
        out_specs=tile,
        out_shape=jax.ShapeDtypeStruct((rows, D), f32),
        compiler_params=pltpu.CompilerParams(
            dimension_semantics=("arbitrary",), vmem_limit_bytes=VMEM_LIMIT),
        name="conv_ffn_final" if final else "conv_ffn",
    )(h, h, g, w_up, conv_w, conv_b, w_down, final_g)


def _rope_tables():
    inv = 1.0 / (10000.0 ** (jnp.arange(0, HD, 2, dtype=f32) / HD))
    pos = (jnp.arange(TP, dtype=jnp.int32) - PAD).astype(f32)
    ang = pos[:, None] * inv[None, :]
    cos = jnp.tile(jnp.cos(ang), (1, 4))
    sin = jnp.tile(jnp.concatenate([-jnp.sin(ang), jnp.sin(ang)], axis=1), (1, 2))
    return cos, sin


def kernel(x, meta_tokens, mix_norm_g, lru_w_in, lru_b_in, lru_conv_w, lru_conv_b, lru_w_a, lru_b_a, lru_w_i, lru_b_i, lru_L, lru_w_out, lru_b_out, attn_w_qkv, attn_lambda_q1, attn_lambda_k1, attn_lambda_q2, attn_lambda_k2, attn_subln_g, attn_w_o, ffn_norm_g, ffn_w_up, ffn_conv_w, ffn_conv_b, ffn_w_down, final_norm_g):
    B = x.shape[0]
    row = lambda v: v.reshape(1, -1)
    meta = jnp.broadcast_to(meta_tokens[None].astype(x.dtype), (B, N_META, D))
    h = jnp.concatenate([jnp.zeros((B, PAD, D), x.dtype), meta, x], axis=1)

    w_gate = jnp.concatenate([lru_w_a[0], lru_w_i[0]], axis=-1).astype(bf16)
    h = _lru_block(h, row(mix_norm_g[0]), lru_w_in[0].astype(bf16), row(lru_b_in[0]),
                   lru_conv_w[0], row(lru_conv_b[0]), w_gate, row(lru_b_a[0]), row(lru_b_i[0]),
                   row(lru_L[0]), lru_w_out[0].astype(bf16), row(lru_b_out[0]))
    h2 = _conv_ffn(h.reshape(B * TP, D), row(ffn_norm_g[0]), ffn_w_up[0].astype(bf16), ffn_conv_w[0],
                   row(ffn_conv_b[0]), ffn_w_down[0].astype(bf16), row(final_norm_g), False)

    lambda_init = 0.8 - 0.6 * math.exp(-0.3 * 1)
    cos, sin = _rope_tables()
    qt, k, vt = _qkv_proj(h2.reshape(B, TP, D), row(mix_norm_g[1]), attn_w_qkv[0].astype(bf16), cos, sin)
    attn = _diff_attention(qt, k, vt, row(attn_lambda_q1[0]), row(attn_lambda_k1[0]),
                           row(attn_lambda_q2[0]), row(attn_lambda_k2[0]), row(attn_subln_g[0]),
                           lambda_init)
    h2 = _out_proj(attn.reshape(B * TP, D), attn_w_o[0].astype(bf16), h2)
    out = _conv_ffn(h2, row(ffn_norm_g[1]), ffn_w_up[1].astype(bf16), ffn_conv_w[1],
                    row(ffn_conv_b[1]), ffn_w_down[1].astype(bf16), row(final_norm_g), True)
    return out.reshape(B, TP, D)[:, PAD + N_META:, :]
```

```python
import functools
import math

import jax
import jax.numpy as jnp
from jax import lax
from jax.experimental import pallas as pl
from jax.experimental.pallas import tpu as pltpu

f32 = jnp.float32
bf16 = jnp.bfloat16

D = 1024
FF = 3072
NH = 8
HW = 128
HD = 64
SEQ = 8192
N_META = 16
PAD = 112
TP = N_META + SEQ + PAD
TM = 640
NT = TP // TM
TQ = 640
TK = 640
QG = 128
NG = TQ // QG
LOG2E = math.log2(math.e)
HALO = 16
FC = 512
EPS = 1e-6
NEG = -1e30
VMEM_LIMIT = 56 * 1024 * 1024


def _rmsnorm(x, g):
    ms = jnp.mean(x * x, axis=-1, keepdims=True)
    return x * lax.rsqrt(ms + EPS) * g


def _gelu_tanh(x):
    c = math.sqrt(2.0 / math.pi)
    return 0.5 * x * (1.0 + jnp.tanh(c * (x + 0.044715 * (x * x * x))))


def _full(shape):
    n = len(shape)
    return pl.BlockSpec(shape, lambda *_: (0,) * n)


def _lru_kernel(h_ref, g_ref, win_ref, bin_ref, cw_ref, cb_ref, wg_ref, ba_ref, bi_ref,
                lam_ref, wout_ref, bout_ref, o_ref, xbuf, a_scr, u_scr, hc):
    c = pl.program_id(1)
    x = h_ref[0]
    hn = _rmsnorm(x, g_ref[...]).astype(bf16)
    u = jnp.dot(hn, win_ref[...], preferred_element_type=f32) + bin_ref[...]
    gate = u[:, :D]
    rec = u[:, D:]

    @pl.when(c == 0)
    def _():
        xbuf[0:8, :] = jnp.zeros((8, D), f32)
        hc[...] = jnp.zeros((8, D), f32)

    xbuf[8:8 + TM, :] = rec
    cw = cw_ref[...]
    conv = (cb_ref[...] + cw[3:4] * rec + cw[2:3] * xbuf[7:7 + TM, :]
            + cw[1:2] * xbuf[6:6 + TM, :] + cw[0:1] * xbuf[5:5 + TM, :])
    xbuf[0:8, :] = xbuf[TM:TM + 8, :]

    cbf = conv.astype(bf16)
    pa, pi = [], []
    for hd in range(NH):
        ga = jnp.dot(cbf[:, hd * HW:(hd + 1) * HW], wg_ref[hd], preferred_element_type=f32)
        pa.append(ga[:, :HW])
        pi.append(ga[:, HW:])
    r = jax.nn.sigmoid(jnp.concatenate(pa, axis=1) + ba_ref[...])
    gi = jax.nn.sigmoid(jnp.concatenate(pi, axis=1) + bi_ref[...])
    lam = lam_ref[...]
    log_sig = jnp.minimum(lam, 0.0) - jnp.log1p(jnp.exp(-jnp.abs(lam)))
    log_a = r * (8.0 * log_sig)
    a = jnp.exp(log_a)
    mult = jnp.sqrt(1.0 - a * a)
    a_scr[...] = a
    u_scr[...] = mult * (gi * conv)

    sub = lax.broadcasted_iota(jnp.int32, (8, D), 0)

    def tile(i, carry):
        off = pl.multiple_of(i * 8, 8)
        av = a_scr[pl.ds(off, 8), :]
        hv = u_scr[pl.ds(off, 8), :]
        for s in (1, 2, 4):
            keep = sub >= s
            a_sh = jnp.where(keep, pltpu.roll(av, s, 0), 1.0)
            h_sh = jnp.where(keep, pltpu.roll(hv, s, 0), 0.0)
            hv = av * h_sh + hv
            av = av * a_sh
        hv = hv + av * carry
        u_scr[pl.ds(off, 8), :] = hv
        return jnp.broadcast_to(hv[7:8, :], (8, D))

    hc[...] = lax.fori_loop(0, TM // 8, tile, hc[...], unroll=4)

    y = (_gelu_tanh(gate) * u_scr[...]).astype(bf16)
    o_ref[0] = x + jnp.dot(y, wout_ref[...], preferred_element_type=f32) + bout_ref[...]


def _lru_block(h, g, w_in, b_in, conv_w, conv_b, w_gate, b_a, b_i, lam, w_out, b_out):
    B = h.shape[0]
    tile = pl.BlockSpec((1, TM, D), lambda b, c: (b, c, 0))
    return pl.pallas_call(
        _lru_kernel,
        grid=(B, NT),
        in_specs=[tile, _full((1, D)), _full((D, 2 * D)), _full((1, 2 * D)), _full((4, D)),
                  _full((1, D)), _full((NH, HW, 2 * HW)), _full((1, D)), _full((1, D)),
                  _full((1, D)), _full((D, D)), _full((1, D))],
        out_specs=tile,
        out_shape=jax.ShapeDtypeStruct((B, TP, D), f32),
        scratch_shapes=[pltpu.VMEM((TM + 8, D), f32), pltpu.VMEM((TM, D), f32),
                        pltpu.VMEM((TM, D), f32), pltpu.VMEM((8, D), f32)],
        compiler_params=pltpu.CompilerParams(
            dimension_semantics=("arbitrary", "arbitrary"), vmem_limit_bytes=VMEM_LIMIT),
        name="lru_block",
    )(h, g, w_in, b_in, conv_w, conv_b, w_gate, b_a, b_i, lam, w_out, b_out)


def _qkv_kernel(h_ref, g_ref, w_ref, cos_ref, sin_ref, qt_ref, k_ref, vt_ref):
    hn = _rmsnorm(h_ref[0], g_ref[...]).astype(bf16)
    qkv = jnp.dot(hn, w_ref[...], preferred_element_type=f32)
    cos = cos_ref[...]
    sin = sin_ref[...]
    lane = lax.broadcasted_iota(jnp.int32, (TM, HW), 1)
    first = (lane & (HD - 1)) < HD // 2

    def rope(t):
        rot = jnp.where(first, pltpu.roll(t, HW - HD // 2, 1), pltpu.roll(t, HD // 2, 1))
        return t * cos + rot * sin

    for hd in range(NH):
        qh = rope(qkv[:, hd * HW:(hd + 1) * HW]) * (HD ** -0.5 * LOG2E)
        qt_ref[0, hd] = qh.T.astype(bf16)
        kh = rope(qkv[:, D + hd * HW:D + (hd + 1) * HW])
        k_ref[0, :, hd * HW:(hd + 1) * HW] = kh.astype(bf16)
        vh = qkv[:, 2 * D + hd * HW:2 * D + (hd + 1) * HW]
        vt_ref[0, hd, 0] = vh.T.astype(bf16)


def _qkv_proj(h, g, w_qkv, cos, sin):
    B = h.shape[0]
    return pl.pallas_call(
        _qkv_kernel,
        grid=(B, NT),
        in_specs=[pl.BlockSpec((1, TM, D), lambda b, c: (b, c, 0)), _full((1, D)),
                  _full((D, 3 * D)),
                  pl.BlockSpec((TM, HW), lambda b, c: (c, 0)),
                  pl.BlockSpec((TM, HW), lambda b, c: (c, 0))],
        out_specs=[pl.BlockSpec((1, NH, HW, TM), lambda b, c: (b, 0, 0, c)),
                   pl.BlockSpec((1, TM, D), lambda b, c: (b, c, 0)),
                   pl.BlockSpec((1, NH, 1, HW, TM), lambda b, c: (b, 0, c, 0, 0))],
        out_shape=[jax.ShapeDtypeStruct((B, NH, HW, TP), bf16),
                   jax.ShapeDtypeStruct((B, TP, D), bf16),
                   jax.ShapeDtypeStruct((B, NH, NT, HW, TM), bf16)],
        compiler_params=pltpu.CompilerParams(
            dimension_semantics=("arbitrary", "arbitrary"), vmem_limit_bytes=VMEM_LIMIT),
        name="qkv_proj",
    )(h, g, w_qkv, cos, sin)


def _attn_kernel(qt_ref, k_ref, vt_ref, lq1_ref, lk1_ref, lq2_ref, lk2_ref, sg_ref, o_ref,
                 q2_scr, sa_scr, sb_scr, m_scr, l_scr, acc_scr, *, lambda_init):
    i = pl.program_id(2)
    feat = lax.broadcasted_iota(jnp.int32, (HW, QG), 0)
    zero = jnp.zeros((HW, QG), bf16)
    for n in range(NG):
        qg = qt_ref[0, 0, :, n * QG:(n + 1) * QG]
        q2_scr[:, 2 * n * QG:(2 * n + 1) * QG] = jnp.where(feat < HD, qg, zero)
        q2_scr[:, (2 * n + 1) * QG:(2 * n + 2) * QG] = jnp.where(feat >= HD, qg, zero)

    m_scr[...] = jnp.full((1, 2 * TQ), NEG, f32)
    l_scr[...] = jnp.zeros((1, 2 * TQ), f32)
    acc_scr[...] = jnp.zeros((HW, 2 * TQ), f32)

    def scores(j, s_ref):
        koff = j * TK if isinstance(j, int) else pl.multiple_of(j * TK, TK)
        kk = k_ref[0, pl.ds(koff, TK), :]
        for n in range(NG):
            s_ref[n] = jnp.dot(kk, q2_scr[:, 2 * n * QG:(2 * n + 2) * QG], preferred_element_type=f32)

    def update(n, j, s_ref, diag):
        cols = slice(2 * n * QG, (2 * n + 2) * QG)
        if diag:
            rows = (n + 1) * QG
            r = lax.broadcasted_iota(jnp.int32, (QG, 2 * QG), 0)
            c = lax.broadcasted_iota(jnp.int32, (QG, 2 * QG), 1) & (QG - 1)
            bot = jnp.where(r <= c, s_ref[n, rows - QG:rows, :], NEG)
            s = bot if rows == QG else jnp.concatenate([s_ref[n, :rows - QG, :], bot], axis=0)
        else:
            rows = TK
            s = s_ref[n]
        m_prev = m_scr[:, cols]
        m_new = jnp.maximum(m_prev, jnp.max(s, axis=0, keepdims=True))
        alpha = jnp.exp2(m_prev - m_new)
        p = jnp.exp2(s - m_new)
        l_scr[:, cols] = alpha * l_scr[:, cols] + jnp.sum(p, axis=0, keepdims=True)
        acc_scr[:, cols] = alpha * acc_scr[:, cols] + jnp.dot(
            vt_ref[0, 0, j, :, :rows], p.astype(bf16), preferred_element_type=f32)
        m_scr[:, cols] = m_new

    def full_step(j, cur, nxt):
        scores(j + 1, nxt)
        for n in range(NG):
            update(n, j, cur, False)

    def diag_step(cur):
        for n in range(NG):
            update(n, i, cur, True)

    scores(0, sa_scr)

    def pair(p, carry):
        full_step(2 * p, sa_scr, sb_scr)
        full_step(2 * p + 1, sb_scr, sa_scr)
        return carry

    lax.fori_loop(0, i // 2, pair, 0)

    @pl.when(i % 2 == 0)
    def _():
        diag_step(sa_scr)

    @pl.when(i % 2 == 1)
    def _():
        full_step(i - 1, sa_scr, sb_scr)
        diag_step(sb_scr)

    lam = (jnp.exp(jnp.sum(lq1_ref[...] * lk1_ref[...], axis=1, keepdims=True))
           - jnp.exp(jnp.sum(lq2_ref[...] * lk2_ref[...], axis=1, keepdims=True)) + lambda_init)
    for n in range(NG):
        c1 = slice(2 * n * QG, (2 * n + 1) * QG)
        c2 = slice((2 * n + 1) * QG, (2 * n + 2) * QG)
        o = acc_scr[:, c1] * (1.0 / l_scr[:, c1]) - lam * (acc_scr[:, c2] * (1.0 / l_scr[:, c2]))
        y = _rmsnorm(o.T, sg_ref[...]) * (1.0 - lambda_init)
        o_ref[0, n * QG:(n + 1) * QG, :] = y.astype(bf16)


def _diff_attention(qt, k, vt, lq1, lk1, lq2, lk2, subln_g, lambda_init):
    B = k.shape[0]
    return pl.pallas_call(
        functools.partial(_attn_kernel, lambda_init=lambda_init),
        grid=(B, NH, TP // TQ),
        in_specs=[pl.BlockSpec((1, 1, HW, TQ), lambda b, h, i: (b, h, 0, i)),
                  pl.BlockSpec((1, TP, HW), lambda b, h, i: (b, 0, h)),
                  pl.BlockSpec((1, 1, NT, HW, TM), lambda b, h, i: (b, h, 0, 0, 0)),
                  _full((1, HD)), _full((1, HD)), _full((1, HD)), _full((1, HD)),
                  _full((1, HW))],
        out_specs=pl.BlockSpec((1, TQ, HW), lambda b, h, i: (b, i, h)),
        out_shape=jax.ShapeDtypeStruct((B, TP, D), bf16),
        scratch_shapes=[pltpu.VMEM((HW, 2 * TQ), bf16), pltpu.VMEM((NG, TK, 2 * QG), f32),
                        pltpu.VMEM((NG, TK, 2 * QG), f32), pltpu.VMEM((1, 2 * TQ), f32),
                        pltpu.VMEM((1, 2 * TQ), f32), pltpu.VMEM((HW, 2 * TQ), f32)],
        compiler_params=pltpu.CompilerParams(
            dimension_semantics=("arbitrary", "arbitrary", "arbitrary"),
            vmem_limit_bytes=VMEM_LIMIT),
        name="diff_attention",
    )(qt, k, vt, lq1, lk1, lq2, lk2, subln_g)


def _oproj_kernel(a_ref, w_ref, h_ref, o_ref):
    o_ref[...] = h_ref[...] + jnp.dot(a_ref[...], w_ref[...], preferred_element_type=f32)


def _out_proj(attn, w_o, h):
    rows = h.shape[0]
    tile = pl.BlockSpec((TM, D), lambda r: (r, 0))
    return pl.pallas_call(
        _oproj_kernel,
        grid=(rows // TM,),
        in_specs=[tile, _full((D, D)), tile],
        out_specs=tile,
        out_shape=jax.ShapeDtypeStruct((rows, D), f32),
        compiler_params=pltpu.CompilerParams(
            dimension_semantics=("arbitrary",), vmem_limit_bytes=VMEM_LIMIT),
        name="attn_out_proj",
    )(attn, w_o, h)


def _ffn_kernel(h_ref, halo_ref, g_ref, wup_ref, cw_ref, cb_ref, wdn_ref, fg_ref, o_ref, *, final):
    c = pl.program_id(0) % NT
    xm = h_ref[...]
    xe = jnp.concatenate([halo_ref[...], xm], axis=0)
    t = c * TM - HALO + lax.broadcasted_iota(jnp.int32, (HALO + TM, 1), 0)
    hn = jnp.where(t >= 0, _rmsnorm(xe, g_ref[...]), 0.0).astype(bf16)

    def conv(u, lo):
        w = cw_ref[:, lo:lo + FC]
        return (cb_ref[:, lo:lo + FC] + w[2:3] * u[HALO:HALO + TM] + w[1:2] * u[HALO - 1:HALO - 1 + TM]
                + w[0:1] * u[HALO - 2:HALO - 2 + TM])

    acc = xm
    for ch in range(FF // FC):
        lo = ch * FC
        ug = jnp.dot(hn, wup_ref[:, lo:lo + FC], preferred_element_type=f32)
        uv = jnp.dot(hn, wup_ref[:, FF + lo:FF + lo + FC], preferred_element_type=f32)
        act = (_gelu_tanh(conv(ug, lo)) * conv(uv, FF + lo)).astype(bf16)
        acc = acc + jnp.dot(act, wdn_ref[lo:lo + FC, :], preferred_element_type=f32)
    if final:
        acc = _rmsnorm(acc, fg_ref[...])
    o_ref[...] = acc


def _conv_ffn(h, g, w_up, conv_w, conv_b, w_down, final_g, final):
    rows = h.shape[0]
    tile = pl.BlockSpec((TM, D), lambda r: (r, 0))
    halo = pl.BlockSpec((HALO, D), lambda r: (jnp.maximum(r * (TM // HALO) - 1, 0), 0))
    return pl.pallas_call(
        functools.partial(_ffn_kernel, final=final),
        grid=(rows // TM,),
        in_specs=[tile, halo, _full((1, D)), _full((D, 2 * FF)), _full((3, 2 * FF)),
                  _full((1, 2 * FF)), _full((FF, D)), _full((1, D))],
        out_specs=tile,
        out_shape=jax.ShapeDtypeStruct((rows, D), f32),
        compiler_params=pltpu.CompilerParams(
            dimension_semantics=("arbitrary",), vmem_limit_bytes=VMEM_LIMIT),
        name="conv_ffn_final" if final else "conv_ffn",
    )(h, h, g, w_up, conv_w, conv_b, w_down, final_g)


def _rope_tables():
    inv = 1.0 / (10000.0 ** (jnp.arange(0, HD, 2, dtype=f32) / HD))
    pos = jnp.arange(TP, dtype=f32)
    ang = pos[:, None] * inv[None, :]
    cos = jnp.tile(jnp.cos(ang), (1, 4))
    sin = jnp.tile(jnp.concatenate([-jnp.sin(ang), jnp.sin(ang)], axis=1), (1, 2))
    return cos, sin


def kernel(x, meta_tokens, mix_norm_g, lru_w_in, lru_b_in, lru_conv_w, lru_conv_b, lru_w_a, lru_b_a, lru_w_i, lru_b_i, lru_L, lru_w_out, lru_b_out, attn_w_qkv, attn_lambda_q1, attn_lambda_k1, attn_lambda_q2, attn_lambda_k2, attn_subln_g, attn_w_o, ffn_norm_g, ffn_w_up, ffn_conv_w, ffn_conv_b, ffn_w_down, final_norm_g):
    B = x.shape[0]
    row = lambda v: v.reshape(1, -1)
    meta = jnp.broadcast_to(meta_tokens[None].astype(x.dtype), (B, N_META, D))
    h = jnp.concatenate([meta, x, jnp.zeros((B, PAD, D), x.dtype)], axis=1)

    w_gate = jnp.concatenate([lru_w_a[0], lru_w_i[0]], axis=-1).astype(bf16)
    h = _lru_block(h, row(mix_norm_g[0]), lru_w_in[0].astype(bf16), row(lru_b_in[0]),
                   lru_conv_w[0], row(lru_conv_b[0]), w_gate, row(lru_b_a[0]), row(lru_b_i[0]),
                   row(lru_L[0]), lru_w_out[0].astype(bf16), row(lru_b_out[0]))
    h2 = _conv_ffn(h.reshape(B * TP, D), row(ffn_norm_g[0]), ffn_w_up[0].astype(bf16), ffn_conv_w[0],
                   row(ffn_conv_b[0]), ffn_w_down[0].astype(bf16), row(final_norm_g), False)

    lambda_init = 0.8 - 0.6 * math.exp(-0.3 * 1)
    cos, sin = _rope_tables()
    qt, k, vt = _qkv_proj(h2.reshape(B, TP, D), row(mix_norm_g[1]), attn_w_qkv[0].astype(bf16), cos, sin)
    attn = _diff_attention(qt, k, vt, row(attn_lambda_q1[0]), row(attn_lambda_k1[0]),
                           row(attn_lambda_q2[0]), row(attn_lambda_k2[0]), row(attn_subln_g[0]),
                           lambda_init)
    h2 = _out_proj(attn.reshape(B * TP, D), attn_w_o[0].astype(bf16), h2)
    out = _conv_ffn(h2, row(ffn_norm_g[1]), ffn_w_up[1].astype(bf16), ffn_conv_w[1],
                    row(ffn_conv_b[1]), ffn_w_down[1].astype(bf16), row(final_norm_g), True)
    return out.reshape(B, TP, D)[:, N_META:N_META + SEQ, :]
```

```python
import functools
import math

import jax
import jax.numpy as jnp
from jax import lax
from jax.experimental import pallas as pl
from jax.experimental.pallas import tpu as pltpu

f32 = jnp.float32
bf16 = jnp.bfloat16

D = 1024
FF = 3072
NH = 8
HW = 128
HD = 64
SEQ = 8192
N_META = 16
PAD = 112
TP = N_META + SEQ + PAD
TM = 640
NT = TP // TM
TQ = 640
TK = 640
QG = 128
NG = TQ // QG
LOG2E = math.log2(math.e)
HALO = 16
FC = 512
EPS = 1e-6
NEG = -1e30
VMEM_LIMIT = 56 * 1024 * 1024


def _rmsnorm(x, g):
    ms = jnp.mean(x * x, axis=-1, keepdims=True)
    return x * lax.rsqrt(ms + EPS) * g


def _gelu_tanh(x):
    k1 = -2.0 * math.sqrt(2.0 / math.pi) * LOG2E
    return x * (1.0 / (1.0 + jnp.exp2(x * (k1 + (k1 * 0.044715) * (x * x)))))


def _full(shape):
    n = len(shape)
    return pl.BlockSpec(shape, lambda *_: (0,) * n)


def _lru_kernel(h_ref, g_ref, win_ref, bin_ref, cw_ref, cb_ref, wg_ref, ba_ref, bi_ref,
                lam_ref, wout_ref, bout_ref, o_ref, xbuf, a_scr, u_scr, hc):
    c = pl.program_id(1)
    x = h_ref[0]
    hn = _rmsnorm(x, g_ref[...]).astype(bf16)
    u = jnp.dot(hn, win_ref[...], preferred_element_type=f32) + bin_ref[...]
    gate = u[:, :D]
    rec = u[:, D:]

    @pl.when(c == 0)
    def _():
        xbuf[0:8, :] = jnp.zeros((8, D), f32)
        hc[...] = jnp.zeros((8, D), f32)

    xbuf[8:8 + TM, :] = rec
    cw = cw_ref[...]
    conv = (cb_ref[...] + cw[3:4] * rec + cw[2:3] * xbuf[7:7 + TM, :]
            + cw[1:2] * xbuf[6:6 + TM, :] + cw[0:1] * xbuf[5:5 + TM, :])
    xbuf[0:8, :] = xbuf[TM:TM + 8, :]

    cbf = conv.astype(bf16)
    pa, pi = [], []
    for hd in range(NH):
        ga = jnp.dot(cbf[:, hd * HW:(hd + 1) * HW], wg_ref[hd], preferred_element_type=f32)
        pa.append(ga[:, :HW])
        pi.append(ga[:, HW:])
    r = jax.nn.sigmoid(jnp.concatenate(pa, axis=1) + ba_ref[...])
    gi = jax.nn.sigmoid(jnp.concatenate(pi, axis=1) + bi_ref[...])
    lam = lam_ref[...]
    log_sig = jnp.minimum(lam, 0.0) - jnp.log1p(jnp.exp(-jnp.abs(lam)))
    log_a = r * (8.0 * log_sig)
    a = jnp.exp(log_a)
    mult = jnp.sqrt(1.0 - a * a)
    a_scr[...] = a
    u_scr[...] = mult * (gi * conv)

    sub = lax.broadcasted_iota(jnp.int32, (8, D), 0)

    def tile(i, carry):
        off = pl.multiple_of(i * 8, 8)
        av = a_scr[pl.ds(off, 8), :]
        hv = u_scr[pl.ds(off, 8), :]
        for s in (1, 2, 4):
            keep = sub >= s
            a_sh = jnp.where(keep, pltpu.roll(av, s, 0), 1.0)
            h_sh = jnp.where(keep, pltpu.roll(hv, s, 0), 0.0)
            hv = av * h_sh + hv
            av = av * a_sh
        hv = hv + av * carry
        u_scr[pl.ds(off, 8), :] = hv
        return jnp.broadcast_to(hv[7:8, :], (8, D))

    hc[...] = lax.fori_loop(0, TM // 8, tile, hc[...], unroll=4)

    y = (_gelu_tanh(gate) * u_scr[...]).astype(bf16)
    o_ref[0] = x + jnp.dot(y, wout_ref[...], preferred_element_type=f32) + bout_ref[...]


def _lru_block(h, g, w_in, b_in, conv_w, conv_b, w_gate, b_a, b_i, lam, w_out, b_out):
    B = h.shape[0]
    tile = pl.BlockSpec((1, TM, D), lambda b, c: (b, c, 0))
    return pl.pallas_call(
        _lru_kernel,
        grid=(B, NT),
        in_specs=[tile, _full((1, D)), _full((D, 2 * D)), _full((1, 2 * D)), _full((4, D)),
                  _full((1, D)), _full((NH, HW, 2 * HW)), _full((1, D)), _full((1, D)),
                  _full((1, D)), _full((D, D)), _full((1, D))],
        out_specs=tile,
        out_shape=jax.ShapeDtypeStruct((B, TP, D), f32),
        scratch_shapes=[pltpu.VMEM((TM + 8, D), f32), pltpu.VMEM((TM, D), f32),
                        pltpu.VMEM((TM, D), f32), pltpu.VMEM((8, D), f32)],
        compiler_params=pltpu.CompilerParams(
            dimension_semantics=("arbitrary", "arbitrary"), vmem_limit_bytes=VMEM_LIMIT),
        name="lru_block",
    )(h, g, w_in, b_in, conv_w, conv_b, w_gate, b_a, b_i, lam, w_out, b_out)


def _qkv_kernel(h_ref, g_ref, w_ref, cos_ref, sin_ref, qt_ref, k_ref, vt_ref):
    hn = _rmsnorm(h_ref[0], g_ref[...]).astype(bf16)
    qkv = jnp.dot(hn, w_ref[...], preferred_element_type=f32)
    cos = cos_ref[...]
    sin = sin_ref[...]
    lane = lax.broadcasted_iota(jnp.int32, (TM, HW), 1)
    first = (lane & (HD - 1)) < HD // 2

    def rope(t):
        rot = jnp.where(first, pltpu.roll(t, HW - HD // 2, 1), pltpu.roll(t, HD // 2, 1))
        return t * cos + rot * sin

    for hd in range(NH):
        qh = rope(qkv[:, hd * HW:(hd + 1) * HW]) * (HD ** -0.5 * LOG2E)
        qt_ref[0, hd] = qh.T.astype(bf16)
        kh = rope(qkv[:, D + hd * HW:D + (hd + 1) * HW])
        k_ref[0, :, hd * HW:(hd + 1) * HW] = kh.astype(bf16)
        vh = qkv[:, 2 * D + hd * HW:2 * D + (hd + 1) * HW]
        vt_ref[0, hd, 0] = vh.T.astype(bf16)


def _qkv_proj(h, g, w_qkv, cos, sin):
    B = h.shape[0]
    return pl.pallas_call(
        _qkv_kernel,
        grid=(B, NT),
        in_specs=[pl.BlockSpec((1, TM, D), lambda b, c: (b, c, 0)), _full((1, D)),
                  _full((D, 3 * D)),
                  pl.BlockSpec((TM, HW), lambda b, c: (c, 0)),
                  pl.BlockSpec((TM, HW), lambda b, c: (c, 0))],
        out_specs=[pl.BlockSpec((1, NH, HW, TM), lambda b, c: (b, 0, 0, c)),
                   pl.BlockSpec((1, TM, D), lambda b, c: (b, c, 0)),
                   pl.BlockSpec((1, NH, 1, HW, TM), lambda b, c: (b, 0, c, 0, 0))],
        out_shape=[jax.ShapeDtypeStruct((B, NH, HW, TP), bf16),
                   jax.ShapeDtypeStruct((B, TP, D), bf16),
                   jax.ShapeDtypeStruct((B, NH, NT, HW, TM), bf16)],
        compiler_params=pltpu.CompilerParams(
            dimension_semantics=("arbitrary", "arbitrary"), vmem_limit_bytes=VMEM_LIMIT),
        name="qkv_proj",
    )(h, g, w_qkv, cos, sin)


def _attn_kernel(qt_ref, k_ref, vt_ref, lq1_ref, lk1_ref, lq2_ref, lk2_ref, sg_ref, o_ref,
                 q2_scr, sa_scr, sb_scr, m_scr, l_scr, acc_scr, *, lambda_init):
    i = pl.program_id(2)
    feat = lax.broadcasted_iota(jnp.int32, (HW, QG), 0)
    zero = jnp.zeros((HW, QG), bf16)
    for n in range(NG):
        qg = qt_ref[0, 0, :, n * QG:(n + 1) * QG]
        q2_scr[:, 2 * n * QG:(2 * n + 1) * QG] = jnp.where(feat < HD, qg, zero)
        q2_scr[:, (2 * n + 1) * QG:(2 * n + 2) * QG] = jnp.where(feat >= HD, qg, zero)

    m_scr[...] = jnp.full((1, 2 * TQ), NEG, f32)
    l_scr[...] = jnp.zeros((1, 2 * TQ), f32)
    acc_scr[...] = jnp.zeros((HW, 2 * TQ), f32)

    def scores(j, s_ref):
        koff = j * TK if isinstance(j, int) else pl.multiple_of(j * TK, TK)
        kk = k_ref[0, pl.ds(koff, TK), :]
        for n in range(NG):
            s_ref[n] = jnp.dot(kk, q2_scr[:, 2 * n * QG:(2 * n + 2) * QG], preferred_element_type=f32)

    def update(n, j, s_ref, diag):
        cols = slice(2 * n * QG, (2 * n + 2) * QG)
        if diag:
            rows = (n + 1) * QG
            r = lax.broadcasted_iota(jnp.int32, (QG, 2 * QG), 0)
            c = lax.broadcasted_iota(jnp.int32, (QG, 2 * QG), 1) & (QG - 1)
            bot = jnp.where(r <= c, s_ref[n, rows - QG:rows, :], NEG)
            s = bot if rows == QG else jnp.concatenate([s_ref[n, :rows - QG, :], bot], axis=0)
        else:
            rows = TK
            s = s_ref[n]
        m_prev = m_scr[:, cols]
        m_new = jnp.maximum(m_prev, jnp.max(s, axis=0, keepdims=True))
        alpha = jnp.exp2(m_prev - m_new)
        p = jnp.exp2(s - m_new)
        l_scr[:, cols] = alpha * l_scr[:, cols] + jnp.sum(p, axis=0, keepdims=True)
        acc_scr[:, cols] = alpha * acc_scr[:, cols] + jnp.dot(
            vt_ref[0, 0, j, :, :rows], p.astype(bf16), preferred_element_type=f32)
        m_scr[:, cols] = m_new

    def full_step(j, cur, nxt):
        scores(j + 1, nxt)
        for n in range(NG):
            update(n, j, cur, False)

    def diag_step(cur):
        for n in range(NG):
            update(n, i, cur, True)

    scores(0, sa_scr)

    def pair(p, carry):
        full_step(2 * p, sa_scr, sb_scr)
        full_step(2 * p + 1, sb_scr, sa_scr)
        return carry

    lax.fori_loop(0, i // 2, pair, 0)

    @pl.when(i % 2 == 0)
    def _():
        diag_step(sa_scr)

    @pl.when(i % 2 == 1)
    def _():
        full_step(i - 1, sa_scr, sb_scr)
        diag_step(sb_scr)

    lam = (jnp.exp(jnp.sum(lq1_ref[...] * lk1_ref[...], axis=1, keepdims=True))
           - jnp.exp(jnp.sum(lq2_ref[...] * lk2_ref[...], axis=1, keepdims=True)) + lambda_init)
    for n in range(NG):
        c1 = slice(2 * n * QG, (2 * n + 1) * QG)
        c2 = slice((2 * n + 1) * QG, (2 * n + 2) * QG)
        o = acc_scr[:, c1] * (1.0 / l_scr[:, c1]) - lam * (acc_scr[:, c2] * (1.0 / l_scr[:, c2]))
        y = _rmsnorm(o.T, sg_ref[...]) * (1.0 - lambda_init)
        o_ref[0, n * QG:(n + 1) * QG, :] = y.astype(bf16)


def _diff_attention(qt, k, vt, lq1, lk1, lq2, lk2, subln_g, lambda_init):
    B = k.shape[0]
    return pl.pallas_call(
        functools.partial(_attn_kernel, lambda_init=lambda_init),
        grid=(B, NH, TP // TQ),
        in_specs=[pl.BlockSpec((1, 1, HW, TQ), lambda b, h, i: (b, h, 0, i)),
                  pl.BlockSpec((1, TP, HW), lambda b, h, i: (b, 0, h)),
                  pl.BlockSpec((1, 1, NT, HW, TM), lambda b, h, i: (b, h, 0, 0, 0)),
                  _full((1, HD)), _full((1, HD)), _full((1, HD)), _full((1, HD)),
                  _full((1, HW))],
        out_specs=pl.BlockSpec((1, TQ, HW), lambda b, h, i: (b, i, h)),
        out_shape=jax.ShapeDtypeStruct((B, TP, D), bf16),
        scratch_shapes=[pltpu.VMEM((HW, 2 * TQ), bf16), pltpu.VMEM((NG, TK, 2 * QG), f32),
                        pltpu.VMEM((NG, TK, 2 * QG), f32), pltpu.VMEM((1, 2 * TQ), f32),
                        pltpu.VMEM((1, 2 * TQ), f32), pltpu.VMEM((HW, 2 * TQ), f32)],
        compiler_params=pltpu.CompilerParams(
            dimension_semantics=("arbitrary", "arbitrary", "arbitrary"),
            vmem_limit_bytes=VMEM_LIMIT),
        name="diff_attention",
    )(qt, k, vt, lq1, lk1, lq2, lk2, subln_g)


def _oproj_kernel(a_ref, w_ref, h_ref, o_ref):
    o_ref[...] = h_ref[...] + jnp.dot(a_ref[...], w_ref[...], preferred_element_type=f32)


def _out_proj(attn, w_o, h):
    rows = h.shape[0]
    tile = pl.BlockSpec((TM, D), lambda r: (r, 0))
    return pl.pallas_call(
        _oproj_kernel,
        grid=(rows // TM,),
        in_specs=[tile, _full((D, D)), tile],
        out_specs=tile,
        out_shape=jax.ShapeDtypeStruct((rows, D), f32),
        compiler_params=pltpu.CompilerParams(
            dimension_semantics=("arbitrary",), vmem_limit_bytes=VMEM_LIMIT),
        name="attn_out_proj",
    )(attn, w_o, h)


def _ffn_kernel(h_ref, halo_ref, g_ref, wup_ref, cw_ref, cb_ref, wdn_ref, fg_ref, o_ref, *, final):
    c = pl.program_id(0) % NT
    xm = h_ref[...]
    xe = jnp.concatenate([halo_ref[...], xm], axis=0)
    t = c * TM - HALO + lax.broadcasted_iota(jnp.int32, (HALO + TM, 1), 0)
    hn = jnp.where(t >= 0, _rmsnorm(xe, g_ref[...]), 0.0).astype(bf16)

    def conv(u, lo):
        w = cw_ref[:, lo:lo + FC]
        return (cb_ref[:, lo:lo + FC] + w[2:3] * u[HALO:HALO + TM] + w[1:2] * u[HALO - 1:HALO - 1 + TM]
                + w[0:1] * u[HALO - 2:HALO - 2 + TM])

    def up(ch):
        lo = ch * FC
        return (jnp.dot(hn, wup_ref[:, lo:lo + FC], preferred_element_type=f32),
                jnp.dot(hn, wup_ref[:, FF + lo:FF + lo + FC], preferred_element_type=f32))

    acc = xm
    nxt = up(0)
    for ch in range(FF // FC):
        lo = ch * FC
        ug, uv = nxt
        if ch + 1 < FF // FC:
            nxt = up(ch + 1)
        act = (_gelu_tanh(conv(ug, lo)) * conv(uv, FF + lo)).astype(bf16)
        acc = acc + jnp.dot(act, wdn_ref[lo:lo + FC, :], preferred_element_type=f32)
    if final:
        acc = _rmsnorm(acc, fg_ref[...])
    o_ref[...] = acc


def _conv_ffn(h, g, w_up, conv_w, conv_b, w_down, final_g, final):
    rows = h.shape[0]
    tile = pl.BlockSpec((TM, D), lambda r: (r, 0))
    halo = pl.BlockSpec((HALO, D), lambda r: (jnp.maximum(r * (TM // HALO) - 1, 0), 0))
    return pl.pallas_call(
        functools.partial(_ffn_kernel, final=final),
        grid=(rows // TM,),
        in_specs=[tile, halo, _full((1, D)), _full((D, 2 * FF)), _full((3, 2 * FF)),
                  _full((1, 2 * FF)), _full((FF, D)), _full((1, D))],
        out_specs=tile,
        out_shape=jax.ShapeDtypeStruct((rows, D), f32),
        compiler_params=pltpu.CompilerParams(
            dimension_semantics=("arbitrary",), vmem_limit_bytes=VMEM_LIMIT),
        name="conv_ffn_final" if final else "conv_ffn",
    )(h, h, g, w_up, conv_w, conv_b, w_down, final_g)


def _rope_tables():
    inv = 1.0 / (10000.0 ** (jnp.arange(0, HD, 2, dtype=f32) / HD))
    pos = jnp.arange(TP, dtype=f32)
    ang = pos[:, None] * inv[None, :]
    cos = jnp.tile(jnp.cos(ang), (1, 4))
    sin = jnp.tile(jnp.concatenate([-jnp.sin(ang), jnp.sin(ang)], axis=1), (1, 2))
    return cos, sin


def kernel(x, meta_tokens, mix_norm_g, lru_w_in, lru_b_in, lru_conv_w, lru_conv_b, lru_w_a, lru_b_a, lru_w_i, lru_b_i, lru_L, lru_w_out, lru_b_out, attn_w_qkv, attn_lambda_q1, attn_lambda_k1, attn_lambda_q2, attn_lambda_k2, attn_subln_g, attn_w_o, ffn_norm_g, ffn_w_up, ffn_conv_w, ffn_conv_b, ffn_w_down, final_norm_g):
    B = x.shape[0]
    row = lambda v: v.reshape(1, -1)
    meta = jnp.broadcast_to(meta_tokens[None].astype(x.dtype), (B, N_META, D))
    h = jnp.concatenate([meta, x, jnp.zeros((B, PAD, D), x.dtype)], axis=1)

    w_gate = jnp.concatenate([lru_w_a[0], lru_w_i[0]], axis=-1).astype(bf16)
    h = _lru_block(h, row(mix_norm_g[0]), lru_w_in[0].astype(bf16), row(lru_b_in[0]),
                   lru_conv_w[0], row(lru_conv_b[0]), w_gate, row(lru_b_a[0]), row(lru_b_i[0]),
                   row(lru_L[0]), lru_w_out[0].astype(bf16), row(lru_b_out[0]))
    h2 = _conv_ffn(h.reshape(B * TP, D), row(ffn_norm_g[0]), ffn_w_up[0].astype(bf16), ffn_conv_w[0],
                   row(ffn_conv_b[0]), ffn_w_down[0].astype(bf16), row(final_norm_g), False)

    lambda_init = 0.8 - 0.6 * math.exp(-0.3 * 1)
    cos, sin = _rope_tables()
    qt, k, vt = _qkv_proj(h2.reshape(B, TP, D), row(mix_norm_g[1]), attn_w_qkv[0].astype(bf16), cos, sin)
    attn = _diff_attention(qt, k, vt, row(attn_lambda_q1[0]), row(attn_lambda_k1[0]),
                           row(attn_lambda_q2[0]), row(attn_lambda_k2[0]), row(attn_subln_g[0]),
                           lambda_init)
    h2 = _out_proj(attn.reshape(B * TP, D), attn_w_o[0].astype(bf16), h2)
    out = _conv_ffn(h2, row(ffn_norm_g[1]), ffn_w_up[1].astype(bf16), ffn_conv_w[1],
                    row(ffn_conv_b[1]), ffn_w_down[1].astype(bf16), row(final_norm_g), True)
    return out.reshape(B, TP, D)[:, N_META:N_META + SEQ, :]
```

```python
import functools
import math

import jax
import jax.numpy as jnp
from jax import lax
from jax.experimental import pallas as pl
from jax.experimental.pallas import tpu as pltpu

f32 = jnp.float32
bf16 = jnp.bfloat16

D = 1024
FF = 3072
NH = 8
HW = 128
HD = 64
SEQ = 8192
N_META = 16
PAD = 112
TP = N_META + SEQ + PAD
TM = 640
NT = TP // TM
TQ = 640
TK = 640
QG = 128
NG = TQ // QG
LOG2E = math.log2(math.e)
TMF = 512
HALO = 16
FC = 512
EPS = 1e-6
NEG = -1e30
VMEM_LIMIT = 56 * 1024 * 1024


def _rmsnorm(x, g):
    ms = jnp.mean(x * x, axis=-1, keepdims=True)
    return x * lax.rsqrt(ms + EPS) * g


def _gelu_tanh(x):
    k1 = -2.0 * math.sqrt(2.0 / math.pi) * LOG2E
    return x * (1.0 / (1.0 + jnp.exp2(x * (k1 + (k1 * 0.044715) * (x * x)))))


def _full(shape):
    n = len(shape)
    return pl.BlockSpec(shape, lambda *_: (0,) * n)


def _lru_kernel(x_ref, meta_ref, g_ref, win_ref, bin_ref, cw_ref, cb_ref, wg_ref, ba_ref, bi_ref,
                lam_ref, wout_ref, bout_ref, o_ref, xbuf, a_scr, u_scr, hc):
    c = pl.program_id(1)
    first = jnp.concatenate([meta_ref[...], x_ref[0:TM - N_META, :]], axis=0)
    body = jnp.where(c == 0, first[0:TM - PAD], x_ref[0:TM - PAD, :])
    tail = jnp.where(c == 0, first[TM - PAD:TM], jnp.where(c == NT - 1, 0.0, x_ref[TM - PAD:TM, :]))
    x = jnp.concatenate([body, tail], axis=0)
    hn = _rmsnorm(x, g_ref[...]).astype(bf16)
    u = jnp.dot(hn, win_ref[...], preferred_element_type=f32) + bin_ref[...]
    gate = u[:, :D]
    rec = u[:, D:]

    @pl.when(c == 0)
    def _():
        xbuf[0:8, :] = jnp.zeros((8, D), f32)
        hc[...] = jnp.zeros((8, D), f32)

    xbuf[8:8 + TM, :] = rec
    cw = cw_ref[...]
    conv = (cb_ref[...] + cw[3:4] * rec + cw[2:3] * xbuf[7:7 + TM, :]
            + cw[1:2] * xbuf[6:6 + TM, :] + cw[0:1] * xbuf[5:5 + TM, :])
    xbuf[0:8, :] = xbuf[TM:TM + 8, :]

    cbf = conv.astype(bf16)
    pa, pi = [], []
    for hd in range(NH):
        ga = jnp.dot(cbf[:, hd * HW:(hd + 1) * HW], wg_ref[hd], preferred_element_type=f32)
        pa.append(ga[:, :HW])
        pi.append(ga[:, HW:])
    r = jax.nn.sigmoid(jnp.concatenate(pa, axis=1) + ba_ref[...])
    gi = jax.nn.sigmoid(jnp.concatenate(pi, axis=1) + bi_ref[...])
    lam = lam_ref[...]
    log_sig = jnp.minimum(lam, 0.0) - jnp.log1p(jnp.exp(-jnp.abs(lam)))
    log_a = r * (8.0 * log_sig)
    a = jnp.exp(log_a)
    mult = jnp.sqrt(1.0 - a * a)
    a_scr[...] = a
    u_scr[...] = mult * (gi * conv)

    sub = lax.broadcasted_iota(jnp.int32, (8, D), 0)

    def tile(i, carry):
        off = pl.multiple_of(i * 8, 8)
        av = a_scr[pl.ds(off, 8), :]
        hv = u_scr[pl.ds(off, 8), :]
        for s in (1, 2, 4):
            keep = sub >= s
            a_sh = jnp.where(keep, pltpu.roll(av, s, 0), 1.0)
            h_sh = jnp.where(keep, pltpu.roll(hv, s, 0), 0.0)
            hv = av * h_sh + hv
            av = av * a_sh
        hv = hv + av * carry
        u_scr[pl.ds(off, 8), :] = hv
        return jnp.broadcast_to(hv[7:8, :], (8, D))

    hc[...] = lax.fori_loop(0, TM // 8, tile, hc[...], unroll=4)

    y = (_gelu_tanh(gate) * u_scr[...]).astype(bf16)
    o_ref[0] = x + jnp.dot(y, wout_ref[...], preferred_element_type=f32) + bout_ref[...]


def _lru_block(x, meta, g, w_in, b_in, conv_w, conv_b, w_gate, b_a, b_i, lam, w_out, b_out):
    B = x.shape[0]
    tile = pl.BlockSpec((1, TM, D), lambda b, c: (b, c, 0))
    window = pl.BlockSpec((None, pl.Element(TM, (0, PAD)), pl.Element(D)),
                          lambda b, c: (b, 8 * jnp.maximum(c * (TM // 8) - N_META // 8, 0), 0))
    return pl.pallas_call(
        _lru_kernel,
        grid=(B, NT),
        in_specs=[window, _full((N_META, D)), _full((1, D)), _full((D, 2 * D)), _full((1, 2 * D)), _full((4, D)),
                  _full((1, D)), _full((NH, HW, 2 * HW)), _full((1, D)), _full((1, D)),
                  _full((1, D)), _full((D, D)), _full((1, D))],
        out_specs=tile,
        out_shape=jax.ShapeDtypeStruct((B, TP, D), f32),
        scratch_shapes=[pltpu.VMEM((TM + 8, D), f32), pltpu.VMEM((TM, D), f32),
                        pltpu.VMEM((TM, D), f32), pltpu.VMEM((8, D), f32)],
        compiler_params=pltpu.CompilerParams(
            dimension_semantics=("arbitrary", "arbitrary"), vmem_limit_bytes=VMEM_LIMIT),
        name="lru_block",
    )(x, meta, g, w_in, b_in, conv_w, conv_b, w_gate, b_a, b_i, lam, w_out, b_out)


def _qkv_kernel(h_ref, g_ref, w_ref, cos_ref, sin_ref, qt_ref, k_ref, vt_ref):
    hn = _rmsnorm(h_ref[0], g_ref[...]).astype(bf16)
    qkv = jnp.dot(hn, w_ref[...], preferred_element_type=f32)
    cos = cos_ref[...]
    sin = sin_ref[...]
    lane = lax.broadcasted_iota(jnp.int32, (TM, HW), 1)
    first = (lane & (HD - 1)) < HD // 2

    def rope(t):
        rot = jnp.where(first, pltpu.roll(t, HW - HD // 2, 1), pltpu.roll(t, HD // 2, 1))
        return t * cos + rot * sin

    for hd in range(NH):
        qh = rope(qkv[:, hd * HW:(hd + 1) * HW]) * (HD ** -0.5 * LOG2E)
        qt_ref[0, hd] = qh.T.astype(bf16)
        kh = rope(qkv[:, D + hd * HW:D + (hd + 1) * HW])
        k_ref[0, :, hd * HW:(hd + 1) * HW] = kh.astype(bf16)
        vh = qkv[:, 2 * D + hd * HW:2 * D + (hd + 1) * HW]
        vt_ref[0, hd, 0] = vh.T.astype(bf16)


def _qkv_proj(h, g, w_qkv, cos, sin):
    B = h.shape[0]
    return pl.pallas_call(
        _qkv_kernel,
        grid=(B, NT),
        in_specs=[pl.BlockSpec((1, TM, D), lambda b, c: (b, c, 0)), _full((1, D)),
                  _full((D, 3 * D)),
                  pl.BlockSpec((TM, HW), lambda b, c: (c, 0)),
                  pl.BlockSpec((TM, HW), lambda b, c: (c, 0))],
        out_specs=[pl.BlockSpec((1, NH, HW, TM), lambda b, c: (b, 0, 0, c)),
                   pl.BlockSpec((1, TM, D), lambda b, c: (b, c, 0)),
                   pl.BlockSpec((1, NH, 1, HW, TM), lambda b, c: (b, 0, c, 0, 0))],
        out_shape=[jax.ShapeDtypeStruct((B, NH, HW, TP), bf16),
                   jax.ShapeDtypeStruct((B, TP, D), bf16),
                   jax.ShapeDtypeStruct((B, NH, NT, HW, TM), bf16)],
        compiler_params=pltpu.CompilerParams(
            dimension_semantics=("arbitrary", "arbitrary"), vmem_limit_bytes=VMEM_LIMIT),
        name="qkv_proj",
    )(h, g, w_qkv, cos, sin)


def _attn_kernel(qt_ref, k_ref, vt_ref, lq1_ref, lk1_ref, lq2_ref, lk2_ref, sg_ref, o_ref,
                 q2_scr, sa_scr, sb_scr, m_scr, l_scr, acc_scr, *, lambda_init):
    i = pl.program_id(2)
    feat = lax.broadcasted_iota(jnp.int32, (HW, QG), 0)
    zero = jnp.zeros((HW, QG), bf16)
    for n in range(NG):
        qg = qt_ref[0, 0, :, n * QG:(n + 1) * QG]
        q2_scr[:, 2 * n * QG:(2 * n + 1) * QG] = jnp.where(feat < HD, qg, zero)
        q2_scr[:, (2 * n + 1) * QG:(2 * n + 2) * QG] = jnp.where(feat >= HD, qg, zero)

    m_scr[...] = jnp.full((1, 2 * TQ), NEG, f32)
    l_scr[...] = jnp.zeros((1, 2 * TQ), f32)
    acc_scr[...] = jnp.zeros((HW, 2 * TQ), f32)

    def scores(j, s_ref):
        koff = j * TK if isinstance(j, int) else pl.multiple_of(j * TK, TK)
        kk = k_ref[0, pl.ds(koff, TK), :]
        for n in range(NG):
            s_ref[n] = jnp.dot(kk, q2_scr[:, 2 * n * QG:(2 * n + 2) * QG], preferred_element_type=f32)

    def update(n, j, s_ref, diag):
        cols = slice(2 * n * QG, (2 * n + 2) * QG)
        if diag:
            rows = (n + 1) * QG
            r = lax.broadcasted_iota(jnp.int32, (QG, 2 * QG), 0)
            c = lax.broadcasted_iota(jnp.int32, (QG, 2 * QG), 1) & (QG - 1)
            bot = jnp.where(r <= c, s_ref[n, rows - QG:rows, :], NEG)
            s = bot if rows == QG else jnp.concatenate([s_ref[n, :rows - QG, :], bot], axis=0)
        else:
            rows = TK
            s = s_ref[n]
        m_prev = m_scr[:, cols]
        m_new = jnp.maximum(m_prev, jnp.max(s, axis=0, keepdims=True))
        alpha = jnp.exp2(m_prev - m_new)
        p = jnp.exp2(s - m_new)
        l_scr[:, cols] = alpha * l_scr[:, cols] + jnp.sum(p, axis=0, keepdims=True)
        acc_scr[:, cols] = alpha * acc_scr[:, cols] + jnp.dot(
            vt_ref[0, 0, j, :, :rows], p.astype(bf16), preferred_element_type=f32)
        m_scr[:, cols] = m_new

    def full_step(j, cur, nxt):
        scores(j + 1, nxt)
        for n in range(NG):
            update(n, j, cur, False)

    def diag_step(cur):
        for n in range(NG):
            update(n, i, cur, True)

    scores(0, sa_scr)

    def pair(p, carry):
        full_step(2 * p, sa_scr, sb_scr)
        full_step(2 * p + 1, sb_scr, sa_scr)
        return carry

    lax.fori_loop(0, i // 2, pair, 0)

    @pl.when(i % 2 == 0)
    def _():
        diag_step(sa_scr)

    @pl.when(i % 2 == 1)
    def _():
        full_step(i - 1, sa_scr, sb_scr)
        diag_step(sb_scr)

    lam = (jnp.exp(jnp.sum(lq1_ref[...] * lk1_ref[...], axis=1, keepdims=True))
           - jnp.exp(jnp.sum(lq2_ref[...] * lk2_ref[...], axis=1, keepdims=True)) + lambda_init)
    for n in range(NG):
        c1 = slice(2 * n * QG, (2 * n + 1) * QG)
        c2 = slice((2 * n + 1) * QG, (2 * n + 2) * QG)
        o = acc_scr[:, c1] * (1.0 / l_scr[:, c1]) - lam * (acc_scr[:, c2] * (1.0 / l_scr[:, c2]))
        y = _rmsnorm(o.T, sg_ref[...]) * (1.0 - lambda_init)
        o_ref[0, n * QG:(n + 1) * QG, :] = y.astype(bf16)


def _diff_attention(qt, k, vt, lq1, lk1, lq2, lk2, subln_g, lambda_init):
    B = k.shape[0]
    return pl.pallas_call(
        functools.partial(_attn_kernel, lambda_init=lambda_init),
        grid=(B, NH, TP // TQ),
        in_specs=[pl.BlockSpec((1, 1, HW, TQ), lambda b, h, i: (b, h, 0, i)),
                  pl.BlockSpec((1, TP, HW), lambda b, h, i: (b, 0, h)),
                  pl.BlockSpec((1, 1, NT, HW, TM), lambda b, h, i: (b, h, 0, 0, 0)),
                  _full((1, HD)), _full((1, HD)), _full((1, HD)), _full((1, HD)),
                  _full((1, HW))],
        out_specs=pl.BlockSpec((1, TQ, HW), lambda b, h, i: (b, i, h)),
        out_shape=jax.ShapeDtypeStruct((B, TP, D), bf16),
        scratch_shapes=[pltpu.VMEM((HW, 2 * TQ), bf16), pltpu.VMEM((NG, TK, 2 * QG), f32),
                        pltpu.VMEM((NG, TK, 2 * QG), f32), pltpu.VMEM((1, 2 * TQ), f32),
                        pltpu.VMEM((1, 2 * TQ), f32), pltpu.VMEM((HW, 2 * TQ), f32)],
        compiler_params=pltpu.CompilerParams(
            dimension_semantics=("arbitrary", "arbitrary", "arbitrary"),
            vmem_limit_bytes=VMEM_LIMIT),
        name="diff_attention",
    )(qt, k, vt, lq1, lk1, lq2, lk2, subln_g)


def _ffn_tile(xe, g_ref, wup_ref, cw_ref, cb_ref, wdn_ref, tm, blank_halo):
    hn = _rmsnorm(xe, g_ref[...])
    if blank_halo is not None:
        row = lax.broadcasted_iota(jnp.int32, (HALO + tm, 1), 0)
        hn = jnp.where(jnp.logical_or(row >= HALO, jnp.logical_not(blank_halo)), hn, 0.0)
    hn = hn.astype(bf16)

    def conv(u, lo):
        w = cw_ref[:, lo:lo + FC]
        return (cb_ref[:, lo:lo + FC] + w[2:3] * u[HALO:HALO + tm] + w[1:2] * u[HALO - 1:HALO - 1 + tm]
                + w[0:1] * u[HALO - 2:HALO - 2 + tm])

    def up(ch):
        lo = ch * FC
        return (jnp.dot(hn, wup_ref[:, lo:lo + FC], preferred_element_type=f32),
                jnp.dot(hn, wup_ref[:, FF + lo:FF + lo + FC], preferred_element_type=f32))

    acc = xe[HALO:]
    nxt = up(0)
    for ch in range(FF // FC):
        lo = ch * FC
        ug, uv = nxt
        if ch + 1 < FF // FC:
            nxt = up(ch + 1)
        act = (_gelu_tanh(conv(ug, lo)) * conv(uv, FF + lo)).astype(bf16)
        acc = acc + jnp.dot(act, wdn_ref[lo:lo + FC, :], preferred_element_type=f32)
    return acc


def _ffn_kernel(h_ref, halo_ref, g_ref, wup_ref, cw_ref, cb_ref, wdn_ref, o_ref):
    xe = jnp.concatenate([halo_ref[...], h_ref[...]], axis=0)
    o_ref[...] = _ffn_tile(xe, g_ref, wup_ref, cw_ref, cb_ref, wdn_ref, TM, pl.program_id(0) % NT == 0)


def _conv_ffn(h, g, w_up, conv_w, conv_b, w_down):
    rows = h.shape[0]
    tile = pl.BlockSpec((TM, D), lambda r: (r, 0))
    halo = pl.BlockSpec((HALO, D), lambda r: (jnp.maximum(r * (TM // HALO) - 1, 0), 0))
    return pl.pallas_call(
        _ffn_kernel,
        grid=(rows // TM,),
        in_specs=[tile, halo, _full((1, D)), _full((D, 2 * FF)), _full((3, 2 * FF)),
                  _full((1, 2 * FF)), _full((FF, D))],
        out_specs=tile,
        out_shape=jax.ShapeDtypeStruct((rows, D), f32),
        compiler_params=pltpu.CompilerParams(
            dimension_semantics=("arbitrary",), vmem_limit_bytes=VMEM_LIMIT),
        name="conv_ffn",
    )(h, h, g, w_up, conv_w, conv_b, w_down)


def _tail_kernel(h_ref, hhalo_ref, a_ref, ahalo_ref, wo_ref, g_ref, wup_ref, cw_ref, cb_ref, wdn_ref,
                 fg_ref, o_ref):
    he = jnp.concatenate([hhalo_ref[...], h_ref[...]], axis=0)
    ae = jnp.concatenate([ahalo_ref[...], a_ref[...]], axis=0)
    xe = he + jnp.dot(ae, wo_ref[...], preferred_element_type=f32)
    y = _ffn_tile(xe, g_ref, wup_ref, cw_ref, cb_ref, wdn_ref, TMF, None)
    o_ref[...] = _rmsnorm(y, fg_ref[...])


def _attn_tail(h, attn, w_o, g, w_up, conv_w, conv_b, w_down, final_g):
    B = h.shape[0] // TP
    nt = SEQ // TMF
    start16 = lambda r: (r // nt) * (TP // HALO) + (r % nt) * (TMF // HALO)
    tile = pl.BlockSpec((pl.Element(TMF), pl.Element(D)), lambda r: (HALO * (start16(r) + 1), 0))
    halo = pl.BlockSpec((pl.Element(HALO), pl.Element(D)), lambda r: (HALO * start16(r), 0))
    return pl.pallas_call(
        _tail_kernel,
        grid=(B * nt,),
        in_specs=[tile, halo, tile, halo, _full((D, D)), _full((1, D)), _full((D, 2 * FF)),
                  _full((3, 2 * FF)), _full((1, 2 * FF)), _full((FF, D)), _full((1, D))],
        out_specs=pl.BlockSpec((None, TMF, D), lambda r: (r // nt, r % nt, 0)),
        out_shape=jax.ShapeDtypeStruct((B, SEQ, D), f32),
        compiler_params=pltpu.CompilerParams(
            dimension_semantics=("arbitrary",), vmem_limit_bytes=VMEM_LIMIT),
        name="attn_tail",
    )(h, h, attn, attn, w_o, g, w_up, conv_w, conv_b, w_down, final_g)


def _rope_tables():
    inv = 1.0 / (10000.0 ** (jnp.arange(0, HD, 2, dtype=f32) / HD))
    pos = jnp.arange(TP, dtype=f32)
    ang = pos[:, None] * inv[None, :]
    cos = jnp.tile(jnp.cos(ang), (1, 4))
    sin = jnp.tile(jnp.concatenate([-jnp.sin(ang), jnp.sin(ang)], axis=1), (1, 2))
    return cos, sin


def kernel(x, meta_tokens, mix_norm_g, lru_w_in, lru_b_in, lru_conv_w, lru_conv_b, lru_w_a, lru_b_a, lru_w_i, lru_b_i, lru_L, lru_w_out, lru_b_out, attn_w_qkv, attn_lambda_q1, attn_lambda_k1, attn_lambda_q2, attn_lambda_k2, attn_subln_g, attn_w_o, ffn_norm_g, ffn_w_up, ffn_conv_w, ffn_conv_b, ffn_w_down, final_norm_g):
    B = x.shape[0]
    row = lambda v: v.reshape(1, -1)

    w_gate = jnp.concatenate([lru_w_a[0], lru_w_i[0]], axis=-1).astype(bf16)
    h = _lru_block(x, meta_tokens.astype(x.dtype), row(mix_norm_g[0]), lru_w_in[0].astype(bf16), row(lru_b_in[0]),
                   lru_conv_w[0], row(lru_conv_b[0]), w_gate, row(lru_b_a[0]), row(lru_b_i[0]),
                   row(lru_L[0]), lru_w_out[0].astype(bf16), row(lru_b_out[0]))
    h2 = _conv_ffn(h.reshape(B * TP, D), row(ffn_norm_g[0]), ffn_w_up[0].astype(bf16), ffn_conv_w[0],
                   row(ffn_conv_b[0]), ffn_w_down[0].astype(bf16))

    lambda_init = 0.8 - 0.6 * math.exp(-0.3 * 1)
    cos, sin = _rope_tables()
    qt, k, vt = _qkv_proj(h2.reshape(B, TP, D), row(mix_norm_g[1]), attn_w_qkv[0].astype(bf16), cos, sin)
    attn = _diff_attention(qt, k, vt, row(attn_lambda_q1[0]), row(attn_lambda_k1[0]),
                           row(attn_lambda_q2[0]), row(attn_lambda_k2[0]), row(attn_subln_g[0]),
                           lambda_init)
    return _attn_tail(h2, attn.reshape(B * TP, D), attn_w_o[0].astype(bf16), row(ffn_norm_g[1]),
                      ffn_w_up[1].astype(bf16), ffn_conv_w[1], row(ffn_conv_b[1]),
                      ffn_w_down[1].astype(bf16), row(final_norm_g))
```

```python
import functools
import math

import jax
import jax.numpy as jnp
from jax import lax
from jax.experimental import pallas as pl
from jax.experimental.pallas import tpu as pltpu

f32 = jnp.float32
bf16 = jnp.bfloat16

D = 1024
FF = 3072
NH = 8
HW = 128
HD = 64
SEQ = 8192
N_META = 16
PAD = 112
TP = N_META + SEQ + PAD
TM = 640
NT = TP // TM
TQ = 640
TK = 640
QG = 128
NG = TQ // QG
UNROLL = 4
LOG2E = math.log2(math.e)
TMF = 512
HALO = 16
FC = 1024
EPS = 1e-6
NEG = -1e30
VMEM_LIMIT = 56 * 1024 * 1024


def _rmsnorm(x, g):
    ms = jnp.mean(x * x, axis=-1, keepdims=True)
    return x * lax.rsqrt(ms + EPS) * g


def _gelu_tanh(x):
    k1 = -2.0 * math.sqrt(2.0 / math.pi) * LOG2E
    return x * (1.0 / (1.0 + jnp.exp2(x * (k1 + (k1 * 0.044715) * (x * x)))))


def _full(shape):
    n = len(shape)
    return pl.BlockSpec(shape, lambda *_: (0,) * n)


def _lru_kernel(x_ref, meta_ref, g_ref, win_ref, bin_ref, cw_ref, cb_ref, wg_ref, ba_ref, bi_ref,
                lam_ref, wout_ref, bout_ref, o_ref, xbuf, a_scr, u_scr, hc):
    c = pl.program_id(1)
    first = jnp.concatenate([meta_ref[...], x_ref[0:TM - N_META, :]], axis=0)
    body = jnp.where(c == 0, first[0:TM - PAD], x_ref[0:TM - PAD, :])
    tail = jnp.where(c == 0, first[TM - PAD:TM], jnp.where(c == NT - 1, 0.0, x_ref[TM - PAD:TM, :]))
    x = jnp.concatenate([body, tail], axis=0)
    hn = _rmsnorm(x, g_ref[...]).astype(bf16)
    u = jnp.dot(hn, win_ref[...], preferred_element_type=f32) + bin_ref[...]
    gate = u[:, :D]
    rec = u[:, D:]

    @pl.when(c == 0)
    def _():
        xbuf[0:8, :] = jnp.zeros((8, D), f32)
        hc[...] = jnp.zeros((8, D), f32)

    xbuf[8:8 + TM, :] = rec
    cw = cw_ref[...]
    conv = (cb_ref[...] + cw[3:4] * rec + cw[2:3] * xbuf[7:7 + TM, :]
            + cw[1:2] * xbuf[6:6 + TM, :] + cw[0:1] * xbuf[5:5 + TM, :])
    xbuf[0:8, :] = xbuf[TM:TM + 8, :]

    cbf = conv.astype(bf16)
    pa, pi = [], []
    for hd in range(NH):
        ga = jnp.dot(cbf[:, hd * HW:(hd + 1) * HW], wg_ref[hd], preferred_element_type=f32)
        pa.append(ga[:, :HW])
        pi.append(ga[:, HW:])
    r = jax.nn.sigmoid(jnp.concatenate(pa, axis=1) + ba_ref[...])
    gi = jax.nn.sigmoid(jnp.concatenate(pi, axis=1) + bi_ref[...])
    lam = lam_ref[...]
    log_sig = jnp.minimum(lam, 0.0) - jnp.log1p(jnp.exp(-jnp.abs(lam)))
    log_a = r * (8.0 * log_sig)
    a = jnp.exp(log_a)
    mult = jnp.sqrt(1.0 - a * a)
    a_scr[...] = a
    u_scr[...] = mult * (gi * conv)

    sub = lax.broadcasted_iota(jnp.int32, (8, D), 0)

    def tile(i, carry):
        off = pl.multiple_of(i * 8, 8)
        av = a_scr[pl.ds(off, 8), :]
        hv = u_scr[pl.ds(off, 8), :]
        for s in (1, 2, 4):
            keep = sub >= s
            a_sh = jnp.where(keep, pltpu.roll(av, s, 0), 1.0)
            h_sh = jnp.where(keep, pltpu.roll(hv, s, 0), 0.0)
            hv = av * h_sh + hv
            av = av * a_sh
        hv = hv + av * carry
        u_scr[pl.ds(off, 8), :] = hv
        return jnp.broadcast_to(hv[7:8, :], (8, D))

    hc[...] = lax.fori_loop(0, TM // 8, tile, hc[...], unroll=4)

    y = (_gelu_tanh(gate) * u_scr[...]).astype(bf16)
    o_ref[0] = x + jnp.dot(y, wout_ref[...], preferred_element_type=f32) + bout_ref[...]


def _lru_block(x, meta, g, w_in, b_in, conv_w, conv_b, w_gate, b_a, b_i, lam, w_out, b_out):
    B = x.shape[0]
    tile = pl.BlockSpec((1, TM, D), lambda b, c: (b, c, 0))
    window = pl.BlockSpec((None, pl.Element(TM, (0, PAD)), pl.Element(D)),
                          lambda b, c: (b, 8 * jnp.maximum(c * (TM // 8) - N_META // 8, 0), 0))
    return pl.pallas_call(
        _lru_kernel,
        grid=(B, NT),
        in_specs=[window, _full((N_META, D)), _full((1, D)), _full((D, 2 * D)), _full((1, 2 * D)), _full((4, D)),
                  _full((1, D)), _full((NH, HW, 2 * HW)), _full((1, D)), _full((1, D)),
                  _full((1, D)), _full((D, D)), _full((1, D))],
        out_specs=tile,
        out_shape=jax.ShapeDtypeStruct((B, TP, D), f32),
        scratch_shapes=[pltpu.VMEM((TM + 8, D), f32), pltpu.VMEM((TM, D), f32),
                        pltpu.VMEM((TM, D), f32), pltpu.VMEM((8, D), f32)],
        compiler_params=pltpu.CompilerParams(
            dimension_semantics=("arbitrary", "arbitrary"), vmem_limit_bytes=VMEM_LIMIT),
        name="lru_block",
    )(x, meta, g, w_in, b_in, conv_w, conv_b, w_gate, b_a, b_i, lam, w_out, b_out)


def _qkv_kernel(h_ref, g_ref, w_ref, cos_ref, sin_ref, qt_ref, k_ref, vt_ref):
    hn = _rmsnorm(h_ref[0], g_ref[...]).astype(bf16)
    qkv = jnp.dot(hn, w_ref[...], preferred_element_type=f32)
    cos = cos_ref[...]
    sin = sin_ref[...]
    lane = lax.broadcasted_iota(jnp.int32, (TM, HW), 1)
    first = (lane & (HD - 1)) < HD // 2

    def rope(t):
        rot = jnp.where(first, pltpu.roll(t, HW - HD // 2, 1), pltpu.roll(t, HD // 2, 1))
        return t * cos + rot * sin

    for hd in range(NH):
        qh = rope(qkv[:, hd * HW:(hd + 1) * HW]) * (HD ** -0.5 * LOG2E)
        qt_ref[0, hd, 0] = qh.T.astype(bf16)
        kh = rope(qkv[:, D + hd * HW:D + (hd + 1) * HW])
        k_ref[0, :, hd * HW:(hd + 1) * HW] = kh.astype(bf16)
        vh = qkv[:, 2 * D + hd * HW:2 * D + (hd + 1) * HW]
        vt_ref[0, hd, 0] = vh.T.astype(bf16)


def _qkv_proj(h, g, w_qkv, cos, sin):
    B = h.shape[0]
    return pl.pallas_call(
        _qkv_kernel,
        grid=(B, NT),
        in_specs=[pl.BlockSpec((1, TM, D), lambda b, c: (b, c, 0)), _full((1, D)),
                  _full((D, 3 * D)),
                  pl.BlockSpec((TM, HW), lambda b, c: (c, 0)),
                  pl.BlockSpec((TM, HW), lambda b, c: (c, 0))],
        out_specs=[pl.BlockSpec((1, NH, 1, HW, TM), lambda b, c: (b, 0, c, 0, 0)),
                   pl.BlockSpec((1, TM, D), lambda b, c: (b, c, 0)),
                   pl.BlockSpec((1, NH, 1, HW, TM), lambda b, c: (b, 0, c, 0, 0))],
        out_shape=[jax.ShapeDtypeStruct((B, NH, NT, HW, TM), bf16),
                   jax.ShapeDtypeStruct((B, TP, D), bf16),
                   jax.ShapeDtypeStruct((B, NH, NT, HW, TM), bf16)],
        compiler_params=pltpu.CompilerParams(
            dimension_semantics=("arbitrary", "arbitrary"), vmem_limit_bytes=VMEM_LIMIT),
        name="qkv_proj",
    )(h, g, w_qkv, cos, sin)


def _attn_kernel(qt_ref, k_ref, vt_ref, lq1_ref, lk1_ref, lq2_ref, lk2_ref, sg_ref, o_ref,
                 q2_scr, sa_scr, sb_scr, m_scr, l_scr, acc_scr, *, lambda_init):
    lam = (jnp.exp(jnp.sum(lq1_ref[...] * lk1_ref[...], axis=1, keepdims=True))
           - jnp.exp(jnp.sum(lq2_ref[...] * lk2_ref[...], axis=1, keepdims=True)) + lambda_init)

    def scores(j, s_ref):
        kk = k_ref[0, pl.ds(pl.multiple_of(j * TK, TK), TK), :]
        for n in range(NG):
            s_ref[n] = jnp.dot(kk, q2_scr[:, 2 * n * QG:(2 * n + 2) * QG], preferred_element_type=f32)

    def update(n, j, s_ref, diag):
        cols = slice(2 * n * QG, (2 * n + 2) * QG)
        if diag:
            rows = (n + 1) * QG
            r = lax.broadcasted_iota(jnp.int32, (QG, 2 * QG), 0)
            c = lax.broadcasted_iota(jnp.int32, (QG, 2 * QG), 1) & (QG - 1)
            bot = jnp.where(r <= c, s_ref[n, rows - QG:rows, :], NEG)
            s = bot if rows == QG else jnp.concatenate([s_ref[n, :rows - QG, :], bot], axis=0)
        else:
            rows = TK
            s = s_ref[n]
        m_prev = m_scr[:, cols]
        m_new = jnp.maximum(m_prev, jnp.max(s, axis=0, keepdims=True))
        alpha = jnp.exp2(m_prev - m_new)
        p = jnp.exp2(s - m_new)
        l_scr[:, cols] = alpha * l_scr[:, cols] + jnp.sum(p, axis=0, keepdims=True)
        acc_scr[:, cols] = alpha * acc_scr[:, cols] + jnp.dot(
            vt_ref[0, 0, j, :, :rows], p.astype(bf16), preferred_element_type=f32)
        m_scr[:, cols] = m_new

    bufs = (sa_scr, sb_scr)

    def full_steps(j0, count):
        for t in range(count):
            scores(j0 + t + 1, bufs[(t + 1) % 2])
            for n in range(NG):
                update(n, j0 + t, bufs[t % 2], False)

    def query_tile(i, carry):
        feat = lax.broadcasted_iota(jnp.int32, (HW, QG), 0)
        zero = jnp.zeros((HW, QG), bf16)
        for n in range(NG):
            qg = qt_ref[0, 0, i, :, n * QG:(n + 1) * QG]
            q2_scr[:, 2 * n * QG:(2 * n + 1) * QG] = jnp.where(feat < HD, qg, zero)
            q2_scr[:, (2 * n + 1) * QG:(2 * n + 2) * QG] = jnp.where(feat >= HD, qg, zero)
        m_scr[...] = jnp.full((1, 2 * TQ), NEG, f32)
        l_scr[...] = jnp.zeros((1, 2 * TQ), f32)
        acc_scr[...] = jnp.zeros((HW, 2 * TQ), f32)

        scores(0, sa_scr)

        def quad(q, c):
            full_steps(UNROLL * q, UNROLL)
            return c

        lax.fori_loop(0, i // UNROLL, quad, 0)

        for rem in range(UNROLL):
            @pl.when(i % UNROLL == rem)
            def _(rem=rem):
                full_steps(i - rem, rem)
                for n in range(NG):
                    update(n, i, bufs[rem % 2], True)

        for n in range(NG):
            c1 = slice(2 * n * QG, (2 * n + 1) * QG)
            c2 = slice((2 * n + 1) * QG, (2 * n + 2) * QG)
            o = acc_scr[:, c1] * (1.0 / l_scr[:, c1]) - lam * (acc_scr[:, c2] * (1.0 / l_scr[:, c2]))
            y = _rmsnorm(o.T, sg_ref[...]) * (1.0 - lambda_init)
            o_ref[0, pl.ds(pl.multiple_of(i * TQ + n * QG, QG), QG), :] = y.astype(bf16)
        return carry

    lax.fori_loop(0, TP // TQ, query_tile, 0)


def _diff_attention(qt, k, vt, lq1, lk1, lq2, lk2, subln_g, lambda_init):
    B = k.shape[0]
    return pl.pallas_call(
        functools.partial(_attn_kernel, lambda_init=lambda_init),
        grid=(B, NH),
        in_specs=[pl.BlockSpec((1, 1, NT, HW, TM), lambda b, h: (b, h, 0, 0, 0)),
                  pl.BlockSpec((1, TP, HW), lambda b, h: (b, 0, h)),
                  pl.BlockSpec((1, 1, NT, HW, TM), lambda b, h: (b, h, 0, 0, 0)),
                  _full((1, HD)), _full((1, HD)), _full((1, HD)), _full((1, HD)),
                  _full((1, HW))],
        out_specs=pl.BlockSpec((1, TP, HW), lambda b, h: (b, 0, h)),
        out_shape=jax.ShapeDtypeStruct((B, TP, D), bf16),
        scratch_shapes=[pltpu.VMEM((HW, 2 * TQ), bf16), pltpu.VMEM((NG, TK, 2 * QG), f32),
                        pltpu.VMEM((NG, TK, 2 * QG), f32), pltpu.VMEM((1, 2 * TQ), f32),
                        pltpu.VMEM((1, 2 * TQ), f32), pltpu.VMEM((HW, 2 * TQ), f32)],
        compiler_params=pltpu.CompilerParams(
            dimension_semantics=("arbitrary", "arbitrary"), vmem_limit_bytes=VMEM_LIMIT),
        name="diff_attention",
    )(qt, k, vt, lq1, lk1, lq2, lk2, subln_g)


def _ffn_tile(xe, g_ref, wup_ref, cw_ref, cb_ref, wdn_ref, tm, blank_halo):
    hn = _rmsnorm(xe, g_ref[...])
    if blank_halo is not None:
        row = lax.broadcasted_iota(jnp.int32, (HALO + tm, 1), 0)
        hn = jnp.where(jnp.logical_or(row >= HALO, jnp.logical_not(blank_halo)), hn, 0.0)
    hn = hn.astype(bf16)

    def conv(u, lo):
        w = cw_ref[:, lo:lo + FC]
        return (cb_ref[:, lo:lo + FC] + w[2:3] * u[HALO:HALO + tm] + w[1:2] * u[HALO - 1:HALO - 1 + tm]
                + w[0:1] * u[HALO - 2:HALO - 2 + tm])

    def up(ch):
        lo = ch * FC
        return (jnp.dot(hn, wup_ref[:, lo:lo + FC], preferred_element_type=f32),
                jnp.dot(hn, wup_ref[:, FF + lo:FF + lo + FC], preferred_element_type=f32))

    acc = xe[HALO:]
    nxt = up(0)
    for ch in range(FF // FC):
        lo = ch * FC
        ug, uv = nxt
        if ch + 1 < FF // FC:
            nxt = up(ch + 1)
        act = (_gelu_tanh(conv(ug, lo)) * conv(uv, FF + lo)).astype(bf16)
        acc = acc + jnp.dot(act, wdn_ref[lo:lo + FC, :], preferred_element_type=f32)
    return acc


def _ffn_kernel(h_ref, halo_ref, g_ref, wup_ref, cw_ref, cb_ref, wdn_ref, o_ref):
    xe = jnp.concatenate([halo_ref[...], h_ref[...]], axis=0)
    o_ref[...] = _ffn_tile(xe, g_ref, wup_ref, cw_ref, cb_ref, wdn_ref, TM, pl.program_id(0) % NT == 0)


def _conv_ffn(h, g, w_up, conv_w, conv_b, w_down):
    rows = h.shape[0]
    tile = pl.BlockSpec((TM, D), lambda r: (r, 0))
    halo = pl.BlockSpec((HALO, D), lambda r: (jnp.maximum(r * (TM // HALO) - 1, 0), 0))
    return pl.pallas_call(
        _ffn_kernel,
        grid=(rows // TM,),
        in_specs=[tile, halo, _full((1, D)), _full((D, 2 * FF)), _full((3, 2 * FF)),
                  _full((1, 2 * FF)), _full((FF, D))],
        out_specs=tile,
        out_shape=jax.ShapeDtypeStruct((rows, D), f32),
        compiler_params=pltpu.CompilerParams(
            dimension_semantics=("arbitrary",), vmem_limit_bytes=VMEM_LIMIT),
        name="conv_ffn",
    )(h, h, g, w_up, conv_w, conv_b, w_down)


def _tail_kernel(h_ref, hhalo_ref, a_ref, ahalo_ref, wo_ref, g_ref, wup_ref, cw_ref, cb_ref, wdn_ref,
                 fg_ref, o_ref):
    he = jnp.concatenate([hhalo_ref[...], h_ref[...]], axis=0)
    ae = jnp.concatenate([ahalo_ref[...], a_ref[...]], axis=0)
    xe = he + jnp.dot(ae, wo_ref[...], preferred_element_type=f32)
    y = _ffn_tile(xe, g_ref, wup_ref, cw_ref, cb_ref, wdn_ref, TMF, None)
    o_ref[...] = _rmsnorm(y, fg_ref[...])


def _attn_tail(h, attn, w_o, g, w_up, conv_w, conv_b, w_down, final_g):
    B = h.shape[0] // TP
    nt = SEQ // TMF
    start16 = lambda r: (r // nt) * (TP // HALO) + (r % nt) * (TMF // HALO)
    tile = pl.BlockSpec((pl.Element(TMF), pl.Element(D)), lambda r: (HALO * (start16(r) + 1), 0))
    halo = pl.BlockSpec((pl.Element(HALO), pl.Element(D)), lambda r: (HALO * start16(r), 0))
    return pl.pallas_call(
        _tail_kernel,
        grid=(B * nt,),
        in_specs=[tile, halo, tile, halo, _full((D, D)), _full((1, D)), _full((D, 2 * FF)),
                  _full((3, 2 * FF)), _full((1, 2 * FF)), _full((FF, D)), _full((1, D))],
        out_specs=pl.BlockSpec((None, TMF, D), lambda r: (r // nt, r % nt, 0)),
        out_shape=jax.ShapeDtypeStruct((B, SEQ, D), f32),
        compiler_params=pltpu.CompilerParams(
            dimension_semantics=("arbitrary",), vmem_limit_bytes=VMEM_LIMIT),
        name="attn_tail",
    )(h, h, attn, attn, w_o, g, w_up, conv_w, conv_b, w_down, final_g)


def _rope_tables():
    inv = 1.0 / (10000.0 ** (jnp.arange(0, HD, 2, dtype=f32) / HD))
    pos = jnp.arange(TP, dtype=f32)
    ang = pos[:, None] * inv[None, :]
    cos = jnp.tile(jnp.cos(ang), (1, 4))
    sin = jnp.tile(jnp.concatenate([-jnp.sin(ang), jnp.sin(ang)], axis=1), (1, 2))
    return cos, sin


def kernel(x, meta_tokens, mix_norm_g, lru_w_in, lru_b_in, lru_conv_w, lru_conv_b, lru_w_a, lru_b_a, lru_w_i, lru_b_i, lru_L, lru_w_out, lru_b_out, attn_w_qkv, attn_lambda_q1, attn_lambda_k1, attn_lambda_q2, attn_lambda_k2, attn_subln_g, attn_w_o, ffn_norm_g, ffn_w_up, ffn_conv_w, ffn_conv_b, ffn_w_down, final_norm_g):
    B = x.shape[0]
    row = lambda v: v.reshape(1, -1)

    w_gate = jnp.concatenate([lru_w_a[0], lru_w_i[0]], axis=-1).astype(bf16)
    h = _lru_block(x, meta_tokens.astype(x.dtype), row(mix_norm_g[0]), lru_w_in[0].astype(bf16), row(lru_b_in[0]),
                   lru_conv_w[0], row(lru_conv_b[0]), w_gate, row(lru_b_a[0]), row(lru_b_i[0]),
                   row(lru_L[0]), lru_w_out[0].astype(bf16), row(lru_b_out[0]))
    h2 = _conv_ffn(h.reshape(B * TP, D), row(ffn_norm_g[0]), ffn_w_up[0].astype(bf16), ffn_conv_w[0],
                   row(ffn_conv_b[0]), ffn_w_down[0].astype(bf16))

    lambda_init = 0.8 - 0.6 * math.exp(-0.3 * 1)
    cos, sin = _rope_tables()
    qt, k, vt = _qkv_proj(h2.reshape(B, TP, D), row(mix_norm_g[1]), attn_w_qkv[0].astype(bf16), cos, sin)
    attn = _diff_attention(qt, k, vt, row(attn_lambda_q1[0]), row(attn_lambda_k1[0]),
                           row(attn_lambda_q2[0]), row(attn_lambda_k2[0]), row(attn_subln_g[0]),
                           lambda_init)
    return _attn_tail(h2, attn.reshape(B * TP, D), attn_w_o[0].astype(bf16), row(ffn_norm_g[1]),
                      ffn_w_up[1].astype(bf16), ffn_conv_w[1], row(ffn_conv_b[1]),
                      ffn_w_down[1].astype(bf16), row(final_norm_g))
```

```python
import functools
import math

import jax
import jax.numpy as jnp
from jax import lax
from jax.experimental import pallas as pl
from jax.experimental.pallas import tpu as pltpu

f32 = jnp.float32
bf16 = jnp.bfloat16

D = 1024
FF = 3072
NH = 8
HW = 128
HD = 64
SEQ = 8192
N_META = 16
PAD = 112
TP = N_META + SEQ + PAD
TM = 640
NT = TP // TM
TQ = 640
TK = 640
QG = 128
NG = TQ // QG
UNROLL = 4
LOG2E = math.log2(math.e)
TMF = 512
HALO = 16
FC = 1024
EPS = 1e-6
NEG = -1e30
VMEM_LIMIT = 56 * 1024 * 1024


def _rmsnorm(x, g):
    ms = jnp.mean(x * x, axis=-1, keepdims=True)
    return x * lax.rsqrt(ms + EPS) * g


def _gelu_tanh(x):
    k1 = -2.0 * math.sqrt(2.0 / math.pi) * LOG2E
    return x * (1.0 / (1.0 + jnp.exp2(x * (k1 + (k1 * 0.044715) * (x * x)))))


def _full(shape):
    n = len(shape)
    return pl.BlockSpec(shape, lambda *_: (0,) * n)


def _lru_kernel(x_ref, xn_ref, meta_ref, g_ref, win_ref, bin_ref, cw_ref, cb_ref, wg_ref, ba_ref, bi_ref,
                lam_ref, wout_ref, bout_ref, o_ref, ua_scr, ub_scr, xbuf, hc):
    c = pl.program_id(1)

    def load_tile(ref, cc):
        first = jnp.concatenate([meta_ref[...], ref[0:TM - N_META, :]], axis=0)
        body = jnp.where(cc == 0, first[0:TM - PAD], ref[0:TM - PAD, :])
        tail = jnp.where(cc == 0, first[TM - PAD:TM], jnp.where(cc == NT - 1, 0.0, ref[TM - PAD:TM, :]))
        return jnp.concatenate([body, tail], axis=0)

    def in_proj(xt, dst):
        hn = _rmsnorm(xt, g_ref[...]).astype(bf16)
        dst[...] = jnp.dot(hn, win_ref[...], preferred_element_type=f32) + bin_ref[...]

    @pl.when(c == 0)
    def _():
        in_proj(load_tile(x_ref, c), ua_scr)
        xbuf[0:8, :] = jnp.zeros((8, D), f32)
        hc[...] = jnp.zeros((8, D), f32)

    def stage(cur, nxt):
        rec = cur[:, D:]
        xbuf[8:8 + TM, :] = rec
        cw = cw_ref[...]
        conv = (cb_ref[...] + cw[3:4] * rec + cw[2:3] * xbuf[7:7 + TM, :]
                + cw[1:2] * xbuf[6:6 + TM, :] + cw[0:1] * xbuf[5:5 + TM, :])
        xbuf[0:8, :] = xbuf[TM:TM + 8, :]

        cbf = conv.astype(bf16)
        pa, pi = [], []
        for hd in range(NH):
            ga = jnp.dot(cbf[:, hd * HW:(hd + 1) * HW], wg_ref[hd], preferred_element_type=f32)
            pa.append(ga[:, :HW])
            pi.append(ga[:, HW:])

        in_proj(load_tile(xn_ref, jnp.minimum(c + 1, NT - 1)), nxt)

        r = jax.nn.sigmoid(jnp.concatenate(pa, axis=1) + ba_ref[...])
        gi = jax.nn.sigmoid(jnp.concatenate(pi, axis=1) + bi_ref[...])
        lam = lam_ref[...]
        log_sig = jnp.minimum(lam, 0.0) - jnp.log1p(jnp.exp(-jnp.abs(lam)))
        a = jnp.exp2(r * (8.0 * LOG2E * log_sig))
        mult = jnp.sqrt(1.0 - a * a)
        u = mult * (gi * conv)

        sub = lax.broadcasted_iota(jnp.int32, (8, D), 0)
        carry = hc[...]
        hs = []
        for i in range(TM // 8):
            av = a[i * 8:(i + 1) * 8]
            hv = u[i * 8:(i + 1) * 8]
            for s in (1, 2, 4):
                keep = sub >= s
                a_sh = jnp.where(keep, pltpu.roll(av, s, 0), 1.0)
                h_sh = jnp.where(keep, pltpu.roll(hv, s, 0), 0.0)
                hv = av * h_sh + hv
                av = av * a_sh
            hv = hv + av * carry
            hs.append(hv)
            carry = jnp.broadcast_to(hv[7:8, :], (8, D))
        hc[...] = carry
        hseq = jnp.concatenate(hs, axis=0)

        y = (_gelu_tanh(cur[:, :D]) * hseq).astype(bf16)
        o_ref[0] = (load_tile(x_ref, c) + jnp.dot(y, wout_ref[...], preferred_element_type=f32)
                    + bout_ref[...])

    @pl.when(c % 2 == 0)
    def _():
        stage(ua_scr, ub_scr)

    @pl.when(c % 2 == 1)
    def _():
        stage(ub_scr, ua_scr)


def _lru_block(x, meta, g, w_in, b_in, conv_w, conv_b, w_gate, b_a, b_i, lam, w_out, b_out):
    B = x.shape[0]
    tile = pl.BlockSpec((1, TM, D), lambda b, c: (b, c, 0))
    start = lambda c: 8 * jnp.maximum(c * (TM // 8) - N_META // 8, 0)
    window = pl.BlockSpec((None, pl.Element(TM, (0, PAD)), pl.Element(D)), lambda b, c: (b, start(c), 0))
    window_next = pl.BlockSpec((None, pl.Element(TM, (0, PAD)), pl.Element(D)),
                               lambda b, c: (b, start(jnp.minimum(c + 1, NT - 1)), 0))
    return pl.pallas_call(
        _lru_kernel,
        grid=(B, NT),
        in_specs=[window, window_next, _full((N_META, D)), _full((1, D)), _full((D, 2 * D)), _full((1, 2 * D)), _full((4, D)),
                  _full((1, D)), _full((NH, HW, 2 * HW)), _full((1, D)), _full((1, D)),
                  _full((1, D)), _full((D, D)), _full((1, D))],
        out_specs=tile,
        out_shape=jax.ShapeDtypeStruct((B, TP, D), f32),
        scratch_shapes=[pltpu.VMEM((TM, 2 * D), f32), pltpu.VMEM((TM, 2 * D), f32),
                        pltpu.VMEM((TM + 8, D), f32), pltpu.VMEM((8, D), f32)],
        compiler_params=pltpu.CompilerParams(
            dimension_semantics=("arbitrary", "arbitrary"), vmem_limit_bytes=VMEM_LIMIT),
        name="lru_block",
    )(x, x, meta, g, w_in, b_in, conv_w, conv_b, w_gate, b_a, b_i, lam, w_out, b_out)


def _qkv_kernel(h_ref, g_ref, w_ref, cos_ref, sin_ref, qt_ref, k_ref, vt_ref):
    hn = _rmsnorm(h_ref[0], g_ref[...]).astype(bf16)
    qkv = jnp.dot(hn, w_ref[...], preferred_element_type=f32)
    cos = cos_ref[...]
    sin = sin_ref[...]
    lane = lax.broadcasted_iota(jnp.int32, (TM, HW), 1)
    first = (lane & (HD - 1)) < HD // 2

    def rope(t):
        rot = jnp.where(first, pltpu.roll(t, HW - HD // 2, 1), pltpu.roll(t, HD // 2, 1))
        return t * cos + rot * sin

    for hd in range(NH):
        qh = rope(qkv[:, hd * HW:(hd + 1) * HW]) * (HD ** -0.5 * LOG2E)
        qt_ref[0, hd, 0] = qh.T.astype(bf16)
        kh = rope(qkv[:, D + hd * HW:D + (hd + 1) * HW])
        k_ref[0, :, hd * HW:(hd + 1) * HW] = kh.astype(bf16)
        vh = qkv[:, 2 * D + hd * HW:2 * D + (hd + 1) * HW]
        vt_ref[0, hd, 0] = vh.T.astype(bf16)


def _qkv_proj(h, g, w_qkv, cos, sin):
    B = h.shape[0]
    return pl.pallas_call(
        _qkv_kernel,
        grid=(B, NT),
        in_specs=[pl.BlockSpec((1, TM, D), lambda b, c: (b, c, 0)), _full((1, D)),
                  _full((D, 3 * D)),
                  pl.BlockSpec((TM, HW), lambda b, c: (c, 0)),
                  pl.BlockSpec((TM, HW), lambda b, c: (c, 0))],
        out_specs=[pl.BlockSpec((1, NH, 1, HW, TM), lambda b, c: (b, 0, c, 0, 0)),
                   pl.BlockSpec((1, TM, D), lambda b, c: (b, c, 0)),
                   pl.BlockSpec((1, NH, 1, HW, TM), lambda b, c: (b, 0, c, 0, 0))],
        out_shape=[jax.ShapeDtypeStruct((B, NH, NT, HW, TM), bf16),
                   jax.ShapeDtypeStruct((B, TP, D), bf16),
                   jax.ShapeDtypeStruct((B, NH, NT, HW, TM), bf16)],
        compiler_params=pltpu.CompilerParams(
            dimension_semantics=("arbitrary", "arbitrary"), vmem_limit_bytes=VMEM_LIMIT),
        name="qkv_proj",
    )(h, g, w_qkv, cos, sin)


def _attn_kernel(qt_ref, k_ref, vt_ref, lq1_ref, lk1_ref, lq2_ref, lk2_ref, sg_ref, o_ref,
                 q2_scr, sa_scr, sb_scr, m_scr, l_scr, acc_scr, *, lambda_init):
    lam = (jnp.exp(jnp.sum(lq1_ref[...] * lk1_ref[...], axis=1, keepdims=True))
           - jnp.exp(jnp.sum(lq2_ref[...] * lk2_ref[...], axis=1, keepdims=True)) + lambda_init)

    def scores(j, s_ref, diag=False):
        koff = pl.multiple_of(j * TK, TK)
        for n in range(NG):
            rows = (n + 1) * QG if diag else TK
            s_ref[n, 0:rows, :] = jnp.dot(k_ref[0, pl.ds(koff, rows), :], q2_scr[:, 2 * n * QG:(2 * n + 2) * QG],
                                          preferred_element_type=f32)

    def update(n, j, s_ref, diag):
        cols = slice(2 * n * QG, (2 * n + 2) * QG)
        if diag:
            rows = (n + 1) * QG
            r = lax.broadcasted_iota(jnp.int32, (QG, 2 * QG), 0)
            c = lax.broadcasted_iota(jnp.int32, (QG, 2 * QG), 1) & (QG - 1)
            bot = jnp.where(r <= c, s_ref[n, rows - QG:rows, :], NEG)
            s = bot if rows == QG else jnp.concatenate([s_ref[n, :rows - QG, :], bot], axis=0)
        else:
            rows = TK
            s = s_ref[n]
        m_prev = m_scr[:, cols]
        m_new = jnp.maximum(m_prev, jnp.max(s, axis=0, keepdims=True))
        alpha = jnp.exp2(m_prev - m_new)
        p = jnp.exp2(s - m_new)
        l_scr[:, cols] = alpha * l_scr[:, cols] + jnp.sum(p, axis=0, keepdims=True)
        acc_scr[:, cols] = alpha * acc_scr[:, cols] + jnp.dot(
            vt_ref[0, 0, j, :, :rows], p.astype(bf16), preferred_element_type=f32)
        m_scr[:, cols] = m_new

    bufs = (sa_scr, sb_scr)

    def full_steps(j0, count, last_is_diag=False):
        for t in range(count):
            scores(j0 + t + 1, bufs[(t + 1) % 2], diag=last_is_diag and t == count - 1)
            for n in range(NG):
                update(n, j0 + t, bufs[t % 2], False)

    def query_tile(i, carry):
        feat = lax.broadcasted_iota(jnp.int32, (HW, QG), 0)
        zero = jnp.zeros((HW, QG), bf16)
        for n in range(NG):
            qg = qt_ref[0, 0, i, :, n * QG:(n + 1) * QG]
            q2_scr[:, 2 * n * QG:(2 * n + 1) * QG] = jnp.where(feat < HD, qg, zero)
            q2_scr[:, (2 * n + 1) * QG:(2 * n + 2) * QG] = jnp.where(feat >= HD, qg, zero)
        m_scr[...] = jnp.full((1, 2 * TQ), NEG, f32)
        l_scr[...] = jnp.zeros((1, 2 * TQ), f32)
        acc_scr[...] = jnp.zeros((HW, 2 * TQ), f32)

        scores(0, sa_scr)

        def quad(q, c):
            full_steps(UNROLL * q, UNROLL)
            return c

        lax.fori_loop(0, i // UNROLL, quad, 0)

        for rem in range(UNROLL):
            @pl.when(i % UNROLL == rem)
            def _(rem=rem):
                full_steps(i - rem, rem, last_is_diag=True)
                for n in range(NG):
                    update(n, i, bufs[rem % 2], True)

        for n in range(NG):
            c1 = slice(2 * n * QG, (2 * n + 1) * QG)
            c2 = slice((2 * n + 1) * QG, (2 * n + 2) * QG)
            o = acc_scr[:, c1] * (1.0 / l_scr[:, c1]) - lam * (acc_scr[:, c2] * (1.0 / l_scr[:, c2]))
            y = _rmsnorm(o.T, sg_ref[...]) * (1.0 - lambda_init)
            o_ref[0, pl.ds(pl.multiple_of(i * TQ + n * QG, QG), QG), :] = y.astype(bf16)
        return carry

    lax.fori_loop(0, TP // TQ, query_tile, 0)


def _diff_attention(qt, k, vt, lq1, lk1, lq2, lk2, subln_g, lambda_init):
    B = k.shape[0]
    return pl.pallas_call(
        functools.partial(_attn_kernel, lambda_init=lambda_init),
        grid=(B, NH),
        in_specs=[pl.BlockSpec((1, 1, NT, HW, TM), lambda b, h: (b, h, 0, 0, 0)),
                  pl.BlockSpec((1, TP, HW), lambda b, h: (b, 0, h)),
                  pl.BlockSpec((1, 1, NT, HW, TM), lambda b, h: (b, h, 0, 0, 0)),
                  _full((1, HD)), _full((1, HD)), _full((1, HD)), _full((1, HD)),
                  _full((1, HW))],
        out_specs=pl.BlockSpec((1, TP, HW), lambda b, h: (b, 0, h)),
        out_shape=jax.ShapeDtypeStruct((B, TP, D), bf16),
        scratch_shapes=[pltpu.VMEM((HW, 2 * TQ), bf16), pltpu.VMEM((NG, TK, 2 * QG), f32),
                        pltpu.VMEM((NG, TK, 2 * QG), f32), pltpu.VMEM((1, 2 * TQ), f32),
                        pltpu.VMEM((1, 2 * TQ), f32), pltpu.VMEM((HW, 2 * TQ), f32)],
        compiler_params=pltpu.CompilerParams(
            dimension_semantics=("arbitrary", "arbitrary"), vmem_limit_bytes=VMEM_LIMIT),
        name="diff_attention",
    )(qt, k, vt, lq1, lk1, lq2, lk2, subln_g)


def _ffn_tile(xe, g_ref, wup_ref, cw_ref, cb_ref, wdn_ref, tm, blank_halo):
    hn = _rmsnorm(xe, g_ref[...])
    if blank_halo is not None:
        row = lax.broadcasted_iota(jnp.int32, (HALO + tm, 1), 0)
        hn = jnp.where(jnp.logical_or(row >= HALO, jnp.logical_not(blank_halo)), hn, 0.0)
    hn = hn.astype(bf16)

    def conv(u, lo):
        w = cw_ref[:, lo:lo + FC]
        return (cb_ref[:, lo:lo + FC] + w[2:3] * u[HALO:HALO + tm] + w[1:2] * u[HALO - 1:HALO - 1 + tm]
                + w[0:1] * u[HALO - 2:HALO - 2 + tm])

    def up(ch):
        lo = ch * FC
        return (jnp.dot(hn, wup_ref[:, lo:lo + FC], preferred_element_type=f32),
                jnp.dot(hn, wup_ref[:, FF + lo:FF + lo + FC], preferred_element_type=f32))

    acc = xe[HALO:]
    nxt = up(0)
    for ch in range(FF // FC):
        lo = ch * FC
        ug, uv = nxt
        if ch + 1 < FF // FC:
            nxt = up(ch + 1)
        act = (_gelu_tanh(conv(ug, lo)) * conv(uv, FF + lo)).astype(bf16)
        acc = acc + jnp.dot(act, wdn_ref[lo:lo + FC, :], preferred_element_type=f32)
    return acc


def _ffn_kernel(h_ref, halo_ref, g_ref, wup_ref, cw_ref, cb_ref, wdn_ref, o_ref):
    xe = jnp.concatenate([halo_ref[...], h_ref[...]], axis=0)
    o_ref[...] = _ffn_tile(xe, g_ref, wup_ref, cw_ref, cb_ref, wdn_ref, TM, pl.program_id(0) % NT == 0)


def _conv_ffn(h, g, w_up, conv_w, conv_b, w_down):
    rows = h.shape[0]
    tile = pl.BlockSpec((TM, D), lambda r: (r, 0))
    halo = pl.BlockSpec((HALO, D), lambda r: (jnp.maximum(r * (TM // HALO) - 1, 0), 0))
    return pl.pallas_call(
        _ffn_kernel,
        grid=(rows // TM,),
        in_specs=[tile, halo, _full((1, D)), _full((D, 2 * FF)), _full((3, 2 * FF)),
                  _full((1, 2 * FF)), _full((FF, D))],
        out_specs=tile,
        out_shape=jax.ShapeDtypeStruct((rows, D), f32),
        compiler_params=pltpu.CompilerParams(
            dimension_semantics=("arbitrary",), vmem_limit_bytes=VMEM_LIMIT),
        name="conv_ffn",
    )(h, h, g, w_up, conv_w, conv_b, w_down)


def _tail_kernel(h_ref, hhalo_ref, a_ref, ahalo_ref, wo_ref, g_ref, wup_ref, cw_ref, cb_ref, wdn_ref,
                 fg_ref, o_ref):
    he = jnp.concatenate([hhalo_ref[...], h_ref[...]], axis=0)
    ae = jnp.concatenate([ahalo_ref[...], a_ref[...]], axis=0)
    xe = he + jnp.dot(ae, wo_ref[...], preferred_element_type=f32)
    y = _ffn_tile(xe, g_ref, wup_ref, cw_ref, cb_ref, wdn_ref, TMF, None)
    o_ref[...] = _rmsnorm(y, fg_ref[...])


def _attn_tail(h, attn, w_o, g, w_up, conv_w, conv_b, w_down, final_g):
    B = h.shape[0] // TP
    nt = SEQ // TMF
    start16 = lambda r: (r // nt) * (TP // HALO) + (r % nt) * (TMF // HALO)
    tile = pl.BlockSpec((pl.Element(TMF), pl.Element(D)), lambda r: (HALO * (start16(r) + 1), 0))
    halo = pl.BlockSpec((pl.Element(HALO), pl.Element(D)), lambda r: (HALO * start16(r), 0))
    return pl.pallas_call(
        _tail_kernel,
        grid=(B * nt,),
        in_specs=[tile, halo, tile, halo, _full((D, D)), _full((1, D)), _full((D, 2 * FF)),
                  _full((3, 2 * FF)), _full((1, 2 * FF)), _full((FF, D)), _full((1, D))],
        out_specs=pl.BlockSpec((None, TMF, D), lambda r: (r // nt, r % nt, 0)),
        out_shape=jax.ShapeDtypeStruct((B, SEQ, D), f32),
        compiler_params=pltpu.CompilerParams(
            dimension_semantics=("arbitrary",), vmem_limit_bytes=VMEM_LIMIT),
        name="attn_tail",
    )(h, h, attn, attn, w_o, g, w_up, conv_w, conv_b, w_down, final_g)


def _rope_tables():
    inv = 1.0 / (10000.0 ** (jnp.arange(0, HD, 2, dtype=f32) / HD))
    pos = jnp.arange(TP, dtype=f32)
    ang = pos[:, None] * inv[None, :]
    cos = jnp.tile(jnp.cos(ang), (1, 4))
    sin = jnp.tile(jnp.concatenate([-jnp.sin(ang), jnp.sin(ang)], axis=1), (1, 2))
    return cos, sin


def kernel(x, meta_tokens, mix_norm_g, lru_w_in, lru_b_in, lru_conv_w, lru_conv_b, lru_w_a, lru_b_a, lru_w_i, lru_b_i, lru_L, lru_w_out, lru_b_out, attn_w_qkv, attn_lambda_q1, attn_lambda_k1, attn_lambda_q2, attn_lambda_k2, attn_subln_g, attn_w_o, ffn_norm_g, ffn_w_up, ffn_conv_w, ffn_conv_b, ffn_w_down, final_norm_g):
    B = x.shape[0]
    row = lambda v: v.reshape(1, -1)

    w_gate = jnp.concatenate([lru_w_a[0], lru_w_i[0]], axis=-1).astype(bf16)
    h = _lru_block(x, meta_tokens.astype(x.dtype), row(mix_norm_g[0]), lru_w_in[0].astype(bf16), row(lru_b_in[0]),
                   lru_conv_w[0], row(lru_conv_b[0]), w_gate, row(lru_b_a[0]), row(lru_b_i[0]),
                   row(lru_L[0]), lru_w_out[0].astype(bf16), row(lru_b_out[0]))
    h2 = _conv_ffn(h.reshape(B * TP, D), row(ffn_norm_g[0]), ffn_w_up[0].astype(bf16), ffn_conv_w[0],
                   row(ffn_conv_b[0]), ffn_w_down[0].astype(bf16))

    lambda_init = 0.8 - 0.6 * math.exp(-0.3 * 1)
    cos, sin = _rope_tables()
    qt, k, vt = _qkv_proj(h2.reshape(B, TP, D), row(mix_norm_g[1]), attn_w_qkv[0].astype(bf16), cos, sin)
    attn = _diff_attention(qt, k, vt, row(attn_lambda_q1[0]), row(attn_lambda_k1[0]),
                           row(attn_lambda_q2[0]), row(attn_lambda_k2[0]), row(attn_subln_g[0]),
                           lambda_init)
    return _attn_tail(h2, attn.reshape(B * TP, D), attn_w_o[0].astype(bf16), row(ffn_norm_g[1]),
                      ffn_w_up[1].astype(bf16), ffn_conv_w[1], row(ffn_conv_b[1]),
                      ffn_w_down[1].astype(bf16), row(final_norm_g))
```

```python
import functools
import math

import jax
import jax.numpy as jnp
from jax import lax
from jax.experimental import pallas as pl
from jax.experimental.pallas import tpu as pltpu

f32 = jnp.float32
bf16 = jnp.bfloat16

D = 1024
FF = 3072
NH = 8
HW = 128
HD = 64
SEQ = 8192
N_META = 16
PAD = 112
TP = N_META + SEQ + PAD
TM = 640
NT = TP // TM
TQ = 640
TK = 640
QG = 128
NG = TQ // QG
UNROLL = 4
LOG2E = math.log2(math.e)
TMF = 512
HALO = 16
FC = 1024
EPS = 1e-6
NEG = -1e30
VMEM_LIMIT = 56 * 1024 * 1024


def _rmsnorm(x, g):
    ms = jnp.mean(x * x, axis=-1, keepdims=True)
    return x * lax.rsqrt(ms + EPS) * g


def _gelu_tanh(x):
    k1 = -2.0 * math.sqrt(2.0 / math.pi) * LOG2E
    return x * (1.0 / (1.0 + jnp.exp2(x * (k1 + (k1 * 0.044715) * (x * x)))))


def _full(shape):
    n = len(shape)
    return pl.BlockSpec(shape, lambda *_: (0,) * n)


def _lru_kernel(x_ref, xn_ref, meta_ref, g_ref, win_ref, bin_ref, cw_ref, cb_ref, wg_ref, ba_ref, bi_ref,
                lam_ref, wout_ref, bout_ref, o_ref, ua_scr, ub_scr, xbuf, hc):
    c = pl.program_id(1)

    def load_tile(ref, cc):
        first = jnp.concatenate([meta_ref[...], ref[0:TM - N_META, :]], axis=0)
        body = jnp.where(cc == 0, first[0:TM - PAD], ref[0:TM - PAD, :])
        tail = jnp.where(cc == 0, first[TM - PAD:TM], jnp.where(cc == NT - 1, 0.0, ref[TM - PAD:TM, :]))
        return jnp.concatenate([body, tail], axis=0)

    def in_proj(xt, dst):
        hn = _rmsnorm(xt, g_ref[...])
        dst[...] = jnp.dot(hn, win_ref[...], preferred_element_type=f32) + bin_ref[...]

    @pl.when(c == 0)
    def _():
        in_proj(load_tile(x_ref, c), ua_scr)
        xbuf[0:8, :] = jnp.zeros((8, D), f32)
        hc[...] = jnp.zeros((8, D), f32)

    def stage(cur, nxt):
        rec = cur[:, D:]
        xbuf[8:8 + TM, :] = rec
        cw = cw_ref[...]
        conv = (cb_ref[...] + cw[3:4] * rec + cw[2:3] * xbuf[7:7 + TM, :]
                + cw[1:2] * xbuf[6:6 + TM, :] + cw[0:1] * xbuf[5:5 + TM, :])
        xbuf[0:8, :] = xbuf[TM:TM + 8, :]

        pa, pi = [], []
        for hd in range(NH):
            ga = jnp.dot(conv[:, hd * HW:(hd + 1) * HW], wg_ref[hd], preferred_element_type=f32)
            pa.append(ga[:, :HW])
            pi.append(ga[:, HW:])

        in_proj(load_tile(xn_ref, jnp.minimum(c + 1, NT - 1)), nxt)

        r = jax.nn.sigmoid(jnp.concatenate(pa, axis=1) + ba_ref[...])
        gi = jax.nn.sigmoid(jnp.concatenate(pi, axis=1) + bi_ref[...])
        lam = lam_ref[...]
        log_sig = jnp.minimum(lam, 0.0) - jnp.log1p(jnp.exp(-jnp.abs(lam)))
        a = jnp.exp2(r * (8.0 * LOG2E * log_sig))
        mult = jnp.sqrt(1.0 - a * a)
        u = mult * (gi * conv)

        sub = lax.broadcasted_iota(jnp.int32, (8, D), 0)
        carry = hc[...]
        hs = []
        for i in range(TM // 8):
            av = a[i * 8:(i + 1) * 8]
            hv = u[i * 8:(i + 1) * 8]
            for s in (1, 2, 4):
                keep = sub >= s
                a_sh = jnp.where(keep, pltpu.roll(av, s, 0), 1.0)
                h_sh = jnp.where(keep, pltpu.roll(hv, s, 0), 0.0)
                hv = av * h_sh + hv
                av = av * a_sh
            hv = hv + av * carry
            hs.append(hv)
            carry = jnp.broadcast_to(hv[7:8, :], (8, D))
        hc[...] = carry
        hseq = jnp.concatenate(hs, axis=0)

        y = _gelu_tanh(cur[:, :D]) * hseq
        o_ref[0] = (load_tile(x_ref, c) + jnp.dot(y, wout_ref[...], preferred_element_type=f32)
                    + bout_ref[...])

    @pl.when(c % 2 == 0)
    def _():
        stage(ua_scr, ub_scr)

    @pl.when(c % 2 == 1)
    def _():
        stage(ub_scr, ua_scr)


def _lru_block(x, meta, g, w_in, b_in, conv_w, conv_b, w_gate, b_a, b_i, lam, w_out, b_out):
    B = x.shape[0]
    tile = pl.BlockSpec((1, TM, D), lambda b, c: (b, c, 0))
    start = lambda c: 8 * jnp.maximum(c * (TM // 8) - N_META // 8, 0)
    window = pl.BlockSpec((None, pl.Element(TM, (0, PAD)), pl.Element(D)), lambda b, c: (b, start(c), 0))
    window_next = pl.BlockSpec((None, pl.Element(TM, (0, PAD)), pl.Element(D)),
                               lambda b, c: (b, start(jnp.minimum(c + 1, NT - 1)), 0))
    return pl.pallas_call(
        _lru_kernel,
        grid=(B, NT),
        in_specs=[window, window_next, _full((N_META, D)), _full((1, D)), _full((D, 2 * D)), _full((1, 2 * D)), _full((4, D)),
                  _full((1, D)), _full((NH, HW, 2 * HW)), _full((1, D)), _full((1, D)),
                  _full((1, D)), _full((D, D)), _full((1, D))],
        out_specs=tile,
        out_shape=jax.ShapeDtypeStruct((B, TP, D), f32),
        scratch_shapes=[pltpu.VMEM((TM, 2 * D), f32), pltpu.VMEM((TM, 2 * D), f32),
                        pltpu.VMEM((TM + 8, D), f32), pltpu.VMEM((8, D), f32)],
        compiler_params=pltpu.CompilerParams(
            dimension_semantics=("arbitrary", "arbitrary"), vmem_limit_bytes=VMEM_LIMIT),
        name="lru_block",
    )(x, x, meta, g, w_in, b_in, conv_w, conv_b, w_gate, b_a, b_i, lam, w_out, b_out)


def _qkv_kernel(h_ref, g_ref, w_ref, cos_ref, sin_ref, qt_ref, k_ref, vt_ref):
    hn = _rmsnorm(h_ref[0], g_ref[...])
    qkv = jnp.dot(hn, w_ref[...], preferred_element_type=f32)
    cos = cos_ref[...]
    sin = sin_ref[...]
    lane = lax.broadcasted_iota(jnp.int32, (TM, HW), 1)
    first = (lane & (HD - 1)) < HD // 2

    def rope(t):
        rot = jnp.where(first, pltpu.roll(t, HW - HD // 2, 1), pltpu.roll(t, HD // 2, 1))
        return t * cos + rot * sin

    for hd in range(NH):
        qh = rope(qkv[:, hd * HW:(hd + 1) * HW]) * (HD ** -0.5 * LOG2E)
        qt_ref[0, hd, 0] = qh.T.astype(bf16)
        kh = rope(qkv[:, D + hd * HW:D + (hd + 1) * HW])
        k_ref[0, :, hd * HW:(hd + 1) * HW] = kh.astype(bf16)
        vh = qkv[:, 2 * D + hd * HW:2 * D + (hd + 1) * HW]
        vt_ref[0, hd, 0] = vh.T.astype(bf16)


def _qkv_proj(h, g, w_qkv, cos, sin):
    B = h.shape[0]
    return pl.pallas_call(
        _qkv_kernel,
        grid=(B, NT),
        in_specs=[pl.BlockSpec((1, TM, D), lambda b, c: (b, c, 0)), _full((1, D)),
                  _full((D, 3 * D)),
                  pl.BlockSpec((TM, HW), lambda b, c: (c, 0)),
                  pl.BlockSpec((TM, HW), lambda b, c: (c, 0))],
        out_specs=[pl.BlockSpec((1, NH, 1, HW, TM), lambda b, c: (b, 0, c, 0, 0)),
                   pl.BlockSpec((1, TM, D), lambda b, c: (b, c, 0)),
                   pl.BlockSpec((1, NH, 1, HW, TM), lambda b, c: (b, 0, c, 0, 0))],
        out_shape=[jax.ShapeDtypeStruct((B, NH, NT, HW, TM), bf16),
                   jax.ShapeDtypeStruct((B, TP, D), bf16),
                   jax.ShapeDtypeStruct((B, NH, NT, HW, TM), bf16)],
        compiler_params=pltpu.CompilerParams(
            dimension_semantics=("arbitrary", "arbitrary"), vmem_limit_bytes=VMEM_LIMIT),
        name="qkv_proj",
    )(h, g, w_qkv, cos, sin)


def _attn_kernel(qt_ref, k_ref, vt_ref, lq1_ref, lk1_ref, lq2_ref, lk2_ref, sg_ref, o_ref,
                 q2_scr, sa_scr, sb_scr, m_scr, l_scr, acc_scr, *, lambda_init):
    lam = (jnp.exp(jnp.sum(lq1_ref[...] * lk1_ref[...], axis=1, keepdims=True))
           - jnp.exp(jnp.sum(lq2_ref[...] * lk2_ref[...], axis=1, keepdims=True)) + lambda_init)

    def scores(j, s_ref, diag=False):
        koff = pl.multiple_of(j * TK, TK)
        for n in range(NG):
            rows = (n + 1) * QG if diag else TK
            s_ref[n, 0:rows, :] = jnp.dot(k_ref[0, pl.ds(koff, rows), :], q2_scr[:, 2 * n * QG:(2 * n + 2) * QG],
                                          preferred_element_type=f32)

    def update(n, j, s_ref, diag):
        cols = slice(2 * n * QG, (2 * n + 2) * QG)
        if diag:
            rows = (n + 1) * QG
            r = lax.broadcasted_iota(jnp.int32, (QG, 2 * QG), 0)
            c = lax.broadcasted_iota(jnp.int32, (QG, 2 * QG), 1) & (QG - 1)
            bot = jnp.where(r <= c, s_ref[n, rows - QG:rows, :], NEG)
            s = bot if rows == QG else jnp.concatenate([s_ref[n, :rows - QG, :], bot], axis=0)
        else:
            rows = TK
            s = s_ref[n]
        m_prev = m_scr[:, cols]
        m_new = jnp.maximum(m_prev, jnp.max(s, axis=0, keepdims=True))
        alpha = jnp.exp2(m_prev - m_new)
        p = jnp.exp2(s - m_new)
        l_scr[:, cols] = alpha * l_scr[:, cols] + jnp.sum(p, axis=0, keepdims=True)
        acc_scr[:, cols] = alpha * acc_scr[:, cols] + jnp.dot(
            vt_ref[0, 0, j, :, :rows], p.astype(bf16), preferred_element_type=f32)
        m_scr[:, cols] = m_new

    bufs = (sa_scr, sb_scr)

    def full_steps(j0, count, last_is_diag=False):
        for t in range(count):
            scores(j0 + t + 1, bufs[(t + 1) % 2], diag=last_is_diag and t == count - 1)
            for n in range(NG):
                update(n, j0 + t, bufs[t % 2], False)

    def query_tile(i, carry):
        feat = lax.broadcasted_iota(jnp.int32, (HW, QG), 0)
        zero = jnp.zeros((HW, QG), bf16)
        for n in range(NG):
            qg = qt_ref[0, 0, i, :, n * QG:(n + 1) * QG]
            q2_scr[:, 2 * n * QG:(2 * n + 1) * QG] = jnp.where(feat < HD, qg, zero)
            q2_scr[:, (2 * n + 1) * QG:(2 * n + 2) * QG] = jnp.where(feat >= HD, qg, zero)
        m_scr[...] = jnp.full((1, 2 * TQ), NEG, f32)
        l_scr[...] = jnp.zeros((1, 2 * TQ), f32)
        acc_scr[...] = jnp.zeros((HW, 2 * TQ), f32)

        scores(0, sa_scr)

        def quad(q, c):
            full_steps(UNROLL * q, UNROLL)
            return c

        lax.fori_loop(0, i // UNROLL, quad, 0)

        for rem in range(UNROLL):
            @pl.when(i % UNROLL == rem)
            def _(rem=rem):
                full_steps(i - rem, rem, last_is_diag=True)
                for n in range(NG):
                    update(n, i, bufs[rem % 2], True)

        for n in range(NG):
            c1 = slice(2 * n * QG, (2 * n + 1) * QG)
            c2 = slice((2 * n + 1) * QG, (2 * n + 2) * QG)
            o = acc_scr[:, c1] * (1.0 / l_scr[:, c1]) - lam * (acc_scr[:, c2] * (1.0 / l_scr[:, c2]))
            y = _rmsnorm(o.T, sg_ref[...]) * (1.0 - lambda_init)
            o_ref[0, pl.ds(pl.multiple_of(i * TQ + n * QG, QG), QG), :] = y.astype(bf16)
        return carry

    lax.fori_loop(0, TP // TQ, query_tile, 0)


def _diff_attention(qt, k, vt, lq1, lk1, lq2, lk2, subln_g, lambda_init):
    B = k.shape[0]
    return pl.pallas_call(
        functools.partial(_attn_kernel, lambda_init=lambda_init),
        grid=(B, NH),
        in_specs=[pl.BlockSpec((1, 1, NT, HW, TM), lambda b, h: (b, h, 0, 0, 0)),
                  pl.BlockSpec((1, TP, HW), lambda b, h: (b, 0, h)),
                  pl.BlockSpec((1, 1, NT, HW, TM), lambda b, h: (b, h, 0, 0, 0)),
                  _full((1, HD)), _full((1, HD)), _full((1, HD)), _full((1, HD)),
                  _full((1, HW))],
        out_specs=pl.BlockSpec((1, TP, HW), lambda b, h: (b, 0, h)),
        out_shape=jax.ShapeDtypeStruct((B, TP, D), bf16),
        scratch_shapes=[pltpu.VMEM((HW, 2 * TQ), bf16), pltpu.VMEM((NG, TK, 2 * QG), f32),
                        pltpu.VMEM((NG, TK, 2 * QG), f32), pltpu.VMEM((1, 2 * TQ), f32),
                        pltpu.VMEM((1, 2 * TQ), f32), pltpu.VMEM((HW, 2 * TQ), f32)],
        compiler_params=pltpu.CompilerParams(
            dimension_semantics=("arbitrary", "arbitrary"), vmem_limit_bytes=VMEM_LIMIT),
        name="diff_attention",
    )(qt, k, vt, lq1, lk1, lq2, lk2, subln_g)


def _ffn_tile(xe, g_ref, wup_ref, cw_ref, cb_ref, wdn_ref, tm, blank_halo):
    hn = _rmsnorm(xe, g_ref[...])
    if blank_halo is not None:
        row = lax.broadcasted_iota(jnp.int32, (HALO + tm, 1), 0)
        hn = jnp.where(jnp.logical_or(row >= HALO, jnp.logical_not(blank_halo)), hn, 0.0)
    hn = hn.astype(bf16)

    def conv(u, lo):
        w = cw_ref[:, lo:lo + FC]
        return (cb_ref[:, lo:lo + FC] + w[2:3] * u[HALO:HALO + tm] + w[1:2] * u[HALO - 1:HALO - 1 + tm]
                + w[0:1] * u[HALO - 2:HALO - 2 + tm])

    def up(ch):
        lo = ch * FC
        return (jnp.dot(hn, wup_ref[:, lo:lo + FC], preferred_element_type=f32),
                jnp.dot(hn, wup_ref[:, FF + lo:FF + lo + FC], preferred_element_type=f32))

    acc = xe[HALO:]
    nxt = up(0)
    for ch in range(FF // FC):
        lo = ch * FC
        ug, uv = nxt
        if ch + 1 < FF // FC:
            nxt = up(ch + 1)
        act = (_gelu_tanh(conv(ug, lo)) * conv(uv, FF + lo)).astype(bf16)
        acc = acc + jnp.dot(act, wdn_ref[lo:lo + FC, :], preferred_element_type=f32)
    return acc


def _ffn_kernel(h_ref, halo_ref, g_ref, wup_ref, cw_ref, cb_ref, wdn_ref, o_ref):
    xe = jnp.concatenate([halo_ref[...], h_ref[...]], axis=0)
    o_ref[...] = _ffn_tile(xe, g_ref, wup_ref, cw_ref, cb_ref, wdn_ref, TM, pl.program_id(0) % NT == 0)


def _conv_ffn(h, g, w_up, conv_w, conv_b, w_down):
    rows = h.shape[0]
    tile = pl.BlockSpec((TM, D), lambda r: (r, 0))
    halo = pl.BlockSpec((HALO, D), lambda r: (jnp.maximum(r * (TM // HALO) - 1, 0), 0))
    return pl.pallas_call(
        _ffn_kernel,
        grid=(rows // TM,),
        in_specs=[tile, halo, _full((1, D)), _full((D, 2 * FF)), _full((3, 2 * FF)),
                  _full((1, 2 * FF)), _full((FF, D))],
        out_specs=tile,
        out_shape=jax.ShapeDtypeStruct((rows, D), f32),
        compiler_params=pltpu.CompilerParams(
            dimension_semantics=("arbitrary",), vmem_limit_bytes=VMEM_LIMIT),
        name="conv_ffn",
    )(h, h, g, w_up, conv_w, conv_b, w_down)


def _tail_kernel(h_ref, hhalo_ref, a_ref, ahalo_ref, wo_ref, g_ref, wup_ref, cw_ref, cb_ref, wdn_ref,
                 fg_ref, o_ref):
    he = jnp.concatenate([hhalo_ref[...], h_ref[...]], axis=0)
    ae = jnp.concatenate([ahalo_ref[...], a_ref[...]], axis=0)
    xe = he + jnp.dot(ae, wo_ref[...], preferred_element_type=f32)
    y = _ffn_tile(xe, g_ref, wup_ref, cw_ref, cb_ref, wdn_ref, TMF, None)
    o_ref[...] = _rmsnorm(y, fg_ref[...])


def _attn_tail(h, attn, w_o, g, w_up, conv_w, conv_b, w_down, final_g):
    B = h.shape[0] // TP
    nt = SEQ // TMF
    start16 = lambda r: (r // nt) * (TP // HALO) + (r % nt) * (TMF // HALO)
    tile = pl.BlockSpec((pl.Element(TMF), pl.Element(D)), lambda r: (HALO * (start16(r) + 1), 0))
    halo = pl.BlockSpec((pl.Element(HALO), pl.Element(D)), lambda r: (HALO * start16(r), 0))
    return pl.pallas_call(
        _tail_kernel,
        grid=(B * nt,),
        in_specs=[tile, halo, tile, halo, _full((D, D)), _full((1, D)), _full((D, 2 * FF)),
                  _full((3, 2 * FF)), _full((1, 2 * FF)), _full((FF, D)), _full((1, D))],
        out_specs=pl.BlockSpec((None, TMF, D), lambda r: (r // nt, r % nt, 0)),
        out_shape=jax.ShapeDtypeStruct((B, SEQ, D), f32),
        compiler_params=pltpu.CompilerParams(
            dimension_semantics=("arbitrary",), vmem_limit_bytes=VMEM_LIMIT),
        name="attn_tail",
    )(h, h, attn, attn, w_o, g, w_up, conv_w, conv_b, w_down, final_g)


def _rope_tables():
    inv = 1.0 / (10000.0 ** (jnp.arange(0, HD, 2, dtype=f32) / HD))
    pos = jnp.arange(TP, dtype=f32)
    ang = pos[:, None] * inv[None, :]
    cos = jnp.tile(jnp.cos(ang), (1, 4))
    sin = jnp.tile(jnp.concatenate([-jnp.sin(ang), jnp.sin(ang)], axis=1), (1, 2))
    return cos, sin


def kernel(x, meta_tokens, mix_norm_g, lru_w_in, lru_b_in, lru_conv_w, lru_conv_b, lru_w_a, lru_b_a, lru_w_i, lru_b_i, lru_L, lru_w_out, lru_b_out, attn_w_qkv, attn_lambda_q1, attn_lambda_k1, attn_lambda_q2, attn_lambda_k2, attn_subln_g, attn_w_o, ffn_norm_g, ffn_w_up, ffn_conv_w, ffn_conv_b, ffn_w_down, final_norm_g):
    B = x.shape[0]
    row = lambda v: v.reshape(1, -1)

    w_gate = jnp.concatenate([lru_w_a[0], lru_w_i[0]], axis=-1)
    h = _lru_block(x, meta_tokens.astype(x.dtype), row(mix_norm_g[0]), lru_w_in[0], row(lru_b_in[0]),
                   lru_conv_w[0], row(lru_conv_b[0]), w_gate, row(lru_b_a[0]), row(lru_b_i[0]),
                   row(lru_L[0]), lru_w_out[0], row(lru_b_out[0]))
    h2 = _conv_ffn(h.reshape(B * TP, D), row(ffn_norm_g[0]), ffn_w_up[0].astype(bf16), ffn_conv_w[0],
                   row(ffn_conv_b[0]), ffn_w_down[0].astype(bf16))

    lambda_init = 0.8 - 0.6 * math.exp(-0.3 * 1)
    cos, sin = _rope_tables()
    qt, k, vt = _qkv_proj(h2.reshape(B, TP, D), row(mix_norm_g[1]), attn_w_qkv[0], cos, sin)
    attn = _diff_attention(qt, k, vt, row(attn_lambda_q1[0]), row(attn_lambda_k1[0]),
                           row(attn_lambda_q2[0]), row(attn_lambda_k2[0]), row(attn_subln_g[0]),
                           lambda_init)
    return _attn_tail(h2, attn.reshape(B * TP, D), attn_w_o[0].astype(bf16), row(ffn_norm_g[1]),
                      ffn_w_up[1].astype(bf16), ffn_conv_w[1], row(ffn_conv_b[1]),
                      ffn_w_down[1].astype(bf16), row(final_norm_g))
```

```python
import functools
import math

import jax
import jax.numpy as jnp
from jax import lax
from jax.experimental import pallas as pl
from jax.experimental.pallas import tpu as pltpu

f32 = jnp.float32
bf16 = jnp.bfloat16

D = 1024
FF = 3072
NH = 8
HW = 128
HD = 64
SEQ = 8192
N_META = 16
PAD = 112
TP = N_META + SEQ + PAD
TM = 640
NT = TP // TM
TQ = 640
TK = 640
QG = 128
NG = TQ // QG
UNROLL = 4
LOG2E = math.log2(math.e)
TMF = 512
HALO = 16
FC = 1024
CAST_STEPS = 4
EPS = 1e-6
NEG = -1e30
VMEM_LIMIT = 56 * 1024 * 1024


def _rmsnorm(x, g):
    ms = jnp.mean(x * x, axis=-1, keepdims=True)
    return x * lax.rsqrt(ms + EPS) * g


def _gelu_tanh(x):
    k1 = -2.0 * math.sqrt(2.0 / math.pi) * LOG2E
    return x * (1.0 / (1.0 + jnp.exp2(x * (k1 + (k1 * 0.044715) * (x * x)))))


def _full(shape):
    n = len(shape)
    return pl.BlockSpec(shape, lambda *_: (0,) * n)


def _lru_kernel(x_ref, xn_ref, meta_ref, g_ref, win_ref, bin_ref, cw_ref, cb_ref, wg_ref, ba_ref, bi_ref,
                lam_ref, wout_ref, bout_ref, o_ref, ua_scr, ub_scr, xbuf, hc):
    c = pl.program_id(1)

    def load_tile(ref, cc):
        first = jnp.concatenate([meta_ref[...], ref[0:TM - N_META, :]], axis=0)
        body = jnp.where(cc == 0, first[0:TM - PAD], ref[0:TM - PAD, :])
        tail = jnp.where(cc == 0, first[TM - PAD:TM], jnp.where(cc == NT - 1, 0.0, ref[TM - PAD:TM, :]))
        return jnp.concatenate([body, tail], axis=0)

    def in_proj(xt, dst):
        hn = _rmsnorm(xt, g_ref[...])
        dst[...] = jnp.dot(hn, win_ref[...], preferred_element_type=f32) + bin_ref[...]

    @pl.when(c == 0)
    def _():
        in_proj(load_tile(x_ref, c), ua_scr)
        xbuf[0:8, :] = jnp.zeros((8, D), f32)
        hc[...] = jnp.zeros((8, D), f32)

    def stage(cur, nxt):
        rec = cur[:, D:]
        xbuf[8:8 + TM, :] = rec
        cw = cw_ref[...]
        conv = (cb_ref[...] + cw[3:4] * rec + cw[2:3] * xbuf[7:7 + TM, :]
                + cw[1:2] * xbuf[6:6 + TM, :] + cw[0:1] * xbuf[5:5 + TM, :])
        xbuf[0:8, :] = xbuf[TM:TM + 8, :]

        pa, pi = [], []
        for hd in range(NH):
            ga = jnp.dot(conv[:, hd * HW:(hd + 1) * HW], wg_ref[hd], preferred_element_type=f32)
            pa.append(ga[:, :HW])
            pi.append(ga[:, HW:])

        in_proj(load_tile(xn_ref, jnp.minimum(c + 1, NT - 1)), nxt)

        r = jax.nn.sigmoid(jnp.concatenate(pa, axis=1) + ba_ref[...])
        gi = jax.nn.sigmoid(jnp.concatenate(pi, axis=1) + bi_ref[...])
        lam = lam_ref[...]
        log_sig = jnp.minimum(lam, 0.0) - jnp.log1p(jnp.exp(-jnp.abs(lam)))
        a = jnp.exp2(r * (8.0 * LOG2E * log_sig))
        mult = jnp.sqrt(1.0 - a * a)
        u = mult * (gi * conv)

        sub = lax.broadcasted_iota(jnp.int32, (8, D), 0)
        carry = hc[...]
        hs = []
        for i in range(TM // 8):
            av = a[i * 8:(i + 1) * 8]
            hv = u[i * 8:(i + 1) * 8]
            for s in (1, 2, 4):
                keep = sub >= s
                a_sh = jnp.where(keep, pltpu.roll(av, s, 0), 1.0)
                h_sh = jnp.where(keep, pltpu.roll(hv, s, 0), 0.0)
                hv = av * h_sh + hv
                av = av * a_sh
            hv = hv + av * carry
            hs.append(hv)
            carry = jnp.broadcast_to(hv[7:8, :], (8, D))
        hc[...] = carry
        hseq = jnp.concatenate(hs, axis=0)

        y = _gelu_tanh(cur[:, :D]) * hseq
        o_ref[0] = (load_tile(x_ref, c) + jnp.dot(y, wout_ref[...], preferred_element_type=f32)
                    + bout_ref[...])

    @pl.when(c % 2 == 0)
    def _():
        stage(ua_scr, ub_scr)

    @pl.when(c % 2 == 1)
    def _():
        stage(ub_scr, ua_scr)


def _lru_block(x, meta, g, w_in, b_in, conv_w, conv_b, w_gate, b_a, b_i, lam, w_out, b_out):
    B = x.shape[0]
    tile = pl.BlockSpec((1, TM, D), lambda b, c: (b, c, 0))
    start = lambda c: 8 * jnp.maximum(c * (TM // 8) - N_META // 8, 0)
    window = pl.BlockSpec((None, pl.Element(TM, (0, PAD)), pl.Element(D)), lambda b, c: (b, start(c), 0))
    window_next = pl.BlockSpec((None, pl.Element(TM, (0, PAD)), pl.Element(D)),
                               lambda b, c: (b, start(jnp.minimum(c + 1, NT - 1)), 0))
    return pl.pallas_call(
        _lru_kernel,
        grid=(B, NT),
        in_specs=[window, window_next, _full((N_META, D)), _full((1, D)), _full((D, 2 * D)), _full((1, 2 * D)), _full((4, D)),
                  _full((1, D)), _full((NH, HW, 2 * HW)), _full((1, D)), _full((1, D)),
                  _full((1, D)), _full((D, D)), _full((1, D))],
        out_specs=tile,
        out_shape=jax.ShapeDtypeStruct((B, TP, D), f32),
        scratch_shapes=[pltpu.VMEM((TM, 2 * D), f32), pltpu.VMEM((TM, 2 * D), f32),
                        pltpu.VMEM((TM + 8, D), f32), pltpu.VMEM((8, D), f32)],
        compiler_params=pltpu.CompilerParams(
            dimension_semantics=("arbitrary", "arbitrary"), vmem_limit_bytes=VMEM_LIMIT),
        name="lru_block",
    )(x, x, meta, g, w_in, b_in, conv_w, conv_b, w_gate, b_a, b_i, lam, w_out, b_out)


def _qkv_kernel(h_ref, g_ref, w_ref, cos_ref, sin_ref, qt_ref, k_ref, vt_ref):
    hn = _rmsnorm(h_ref[0], g_ref[...])
    qkv = jnp.dot(hn, w_ref[...], preferred_element_type=f32)
    cos = cos_ref[...]
    sin = sin_ref[...]
    lane = lax.broadcasted_iota(jnp.int32, (TM, HW), 1)
    first = (lane & (HD - 1)) < HD // 2

    def rope(t):
        rot = jnp.where(first, pltpu.roll(t, HW - HD // 2, 1), pltpu.roll(t, HD // 2, 1))
        return t * cos + rot * sin

    for hd in range(NH):
        qh = rope(qkv[:, hd * HW:(hd + 1) * HW]) * (HD ** -0.5 * LOG2E)
        qt_ref[0, hd, 0] = qh.T.astype(bf16)
        kh = rope(qkv[:, D + hd * HW:D + (hd + 1) * HW])
        k_ref[0, :, hd * HW:(hd + 1) * HW] = kh.astype(bf16)
        vh = qkv[:, 2 * D + hd * HW:2 * D + (hd + 1) * HW]
        vt_ref[0, hd, 0] = vh.T.astype(bf16)


def _qkv_proj(h, g, w_qkv, cos, sin):
    B = h.shape[0]
    return pl.pallas_call(
        _qkv_kernel,
        grid=(B, NT),
        in_specs=[pl.BlockSpec((1, TM, D), lambda b, c: (b, c, 0)), _full((1, D)),
                  _full((D, 3 * D)),
                  pl.BlockSpec((TM, HW), lambda b, c: (c, 0)),
                  pl.BlockSpec((TM, HW), lambda b, c: (c, 0))],
        out_specs=[pl.BlockSpec((1, NH, 1, HW, TM), lambda b, c: (b, 0, c, 0, 0)),
                   pl.BlockSpec((1, TM, D), lambda b, c: (b, c, 0)),
                   pl.BlockSpec((1, NH, 1, HW, TM), lambda b, c: (b, 0, c, 0, 0))],
        out_shape=[jax.ShapeDtypeStruct((B, NH, NT, HW, TM), bf16),
                   jax.ShapeDtypeStruct((B, TP, D), bf16),
                   jax.ShapeDtypeStruct((B, NH, NT, HW, TM), bf16)],
        compiler_params=pltpu.CompilerParams(
            dimension_semantics=("arbitrary", "arbitrary"), vmem_limit_bytes=VMEM_LIMIT),
        name="qkv_proj",
    )(h, g, w_qkv, cos, sin)


def _attn_kernel(qt_ref, k_ref, vt_ref, lq1_ref, lk1_ref, lq2_ref, lk2_ref, sg_ref, o_ref,
                 q2_scr, sa_scr, sb_scr, m_scr, l_scr, acc_scr, *, lambda_init):
    lam = (jnp.exp(jnp.sum(lq1_ref[...] * lk1_ref[...], axis=1, keepdims=True))
           - jnp.exp(jnp.sum(lq2_ref[...] * lk2_ref[...], axis=1, keepdims=True)) + lambda_init)

    def scores(j, s_ref, diag=False):
        koff = pl.multiple_of(j * TK, TK)
        for n in range(NG):
            rows = (n + 1) * QG if diag else TK
            s_ref[n, 0:rows, :] = jnp.dot(k_ref[0, pl.ds(koff, rows), :], q2_scr[:, 2 * n * QG:(2 * n + 2) * QG],
                                          preferred_element_type=f32)

    def update(n, j, s_ref, diag):
        cols = slice(2 * n * QG, (2 * n + 2) * QG)
        if diag:
            rows = (n + 1) * QG
            r = lax.broadcasted_iota(jnp.int32, (QG, 2 * QG), 0)
            c = lax.broadcasted_iota(jnp.int32, (QG, 2 * QG), 1) & (QG - 1)
            bot = jnp.where(r <= c, s_ref[n, rows - QG:rows, :], NEG)
            s = bot if rows == QG else jnp.concatenate([s_ref[n, :rows - QG, :], bot], axis=0)
        else:
            rows = TK
            s = s_ref[n]
        m_prev = m_scr[:, cols]
        m_new = jnp.maximum(m_prev, jnp.max(s, axis=0, keepdims=True))
        alpha = jnp.exp2(m_prev - m_new)
        p = jnp.exp2(s - m_new)
        l_scr[:, cols] = alpha * l_scr[:, cols] + jnp.sum(p, axis=0, keepdims=True)
        acc_scr[:, cols] = alpha * acc_scr[:, cols] + jnp.dot(
            vt_ref[0, 0, j, :, :rows], p.astype(bf16), preferred_element_type=f32)
        m_scr[:, cols] = m_new

    bufs = (sa_scr, sb_scr)

    def full_steps(j0, count, last_is_diag=False):
        for t in range(count):
            scores(j0 + t + 1, bufs[(t + 1) % 2], diag=last_is_diag and t == count - 1)
            for n in range(NG):
                update(n, j0 + t, bufs[t % 2], False)

    def query_tile(i, carry):
        feat = lax.broadcasted_iota(jnp.int32, (HW, QG), 0)
        zero = jnp.zeros((HW, QG), bf16)
        for n in range(NG):
            qg = qt_ref[0, 0, i, :, n * QG:(n + 1) * QG]
            q2_scr[:, 2 * n * QG:(2 * n + 1) * QG] = jnp.where(feat < HD, qg, zero)
            q2_scr[:, (2 * n + 1) * QG:(2 * n + 2) * QG] = jnp.where(feat >= HD, qg, zero)
        m_scr[...] = jnp.full((1, 2 * TQ), NEG, f32)
        l_scr[...] = jnp.zeros((1, 2 * TQ), f32)
        acc_scr[...] = jnp.zeros((HW, 2 * TQ), f32)

        scores(0, sa_scr)

        def quad(q, c):
            full_steps(UNROLL * q, UNROLL)
            return c

        lax.fori_loop(0, i // UNROLL, quad, 0)

        for rem in range(UNROLL):
            @pl.when(i % UNROLL == rem)
            def _(rem=rem):
                full_steps(i - rem, rem, last_is_diag=True)
                for n in range(NG):
                    update(n, i, bufs[rem % 2], True)

        for n in range(NG):
            c1 = slice(2 * n * QG, (2 * n + 1) * QG)
            c2 = slice((2 * n + 1) * QG, (2 * n + 2) * QG)
            o = acc_scr[:, c1] * (1.0 / l_scr[:, c1]) - lam * (acc_scr[:, c2] * (1.0 / l_scr[:, c2]))
            y = _rmsnorm(o.T, sg_ref[...]) * (1.0 - lambda_init)
            o_ref[0, pl.ds(pl.multiple_of(i * TQ + n * QG, QG), QG), :] = y.astype(bf16)
        return carry

    lax.fori_loop(0, TP // TQ, query_tile, 0)


def _diff_attention(qt, k, vt, lq1, lk1, lq2, lk2, subln_g, lambda_init):
    B = k.shape[0]
    return pl.pallas_call(
        functools.partial(_attn_kernel, lambda_init=lambda_init),
        grid=(B, NH),
        in_specs=[pl.BlockSpec((1, 1, NT, HW, TM), lambda b, h: (b, h, 0, 0, 0)),
                  pl.BlockSpec((1, TP, HW), lambda b, h: (b, 0, h)),
                  pl.BlockSpec((1, 1, NT, HW, TM), lambda b, h: (b, h, 0, 0, 0)),
                  _full((1, HD)), _full((1, HD)), _full((1, HD)), _full((1, HD)),
                  _full((1, HW))],
        out_specs=pl.BlockSpec((1, TP, HW), lambda b, h: (b, 0, h)),
        out_shape=jax.ShapeDtypeStruct((B, TP, D), bf16),
        scratch_shapes=[pltpu.VMEM((HW, 2 * TQ), bf16), pltpu.VMEM((NG, TK, 2 * QG), f32),
                        pltpu.VMEM((NG, TK, 2 * QG), f32), pltpu.VMEM((1, 2 * TQ), f32),
                        pltpu.VMEM((1, 2 * TQ), f32), pltpu.VMEM((HW, 2 * TQ), f32)],
        compiler_params=pltpu.CompilerParams(
            dimension_semantics=("arbitrary", "arbitrary"), vmem_limit_bytes=VMEM_LIMIT),
        name="diff_attention",
    )(qt, k, vt, lq1, lk1, lq2, lk2, subln_g)


def _ffn_tile(xe, g_ref, wup_ref, cw_ref, cb_ref, wdn_ref, tm, blank_halo):
    hn = _rmsnorm(xe, g_ref[...])
    if blank_halo is not None:
        row = lax.broadcasted_iota(jnp.int32, (HALO + tm, 1), 0)
        hn = jnp.where(jnp.logical_or(row >= HALO, jnp.logical_not(blank_halo)), hn, 0.0)
    hn = hn.astype(bf16)

    def conv(u, lo):
        w = cw_ref[:, lo:lo + FC]
        return (cb_ref[:, lo:lo + FC] + w[2:3] * u[HALO:HALO + tm] + w[1:2] * u[HALO - 1:HALO - 1 + tm]
                + w[0:1] * u[HALO - 2:HALO - 2 + tm])

    def up(ch):
        lo = ch * FC
        return (jnp.dot(hn, wup_ref[:, lo:lo + FC], preferred_element_type=f32),
                jnp.dot(hn, wup_ref[:, FF + lo:FF + lo + FC], preferred_element_type=f32))

    acc = xe[HALO:]
    nxt = up(0)
    for ch in range(FF // FC):
        lo = ch * FC
        ug, uv = nxt
        if ch + 1 < FF // FC:
            nxt = up(ch + 1)
        act = (_gelu_tanh(conv(ug, lo)) * conv(uv, FF + lo)).astype(bf16)
        acc = acc + jnp.dot(act, wdn_ref[lo:lo + FC, :], preferred_element_type=f32)
    return acc


def _ffn_kernel(h_ref, halo_ref, g_ref, wup_ref, cw_ref, cb_ref, wdn_ref, o_ref):
    xe = jnp.concatenate([halo_ref[...], h_ref[...]], axis=0)
    o_ref[...] = _ffn_tile(xe, g_ref, wup_ref, cw_ref, cb_ref, wdn_ref, TM, pl.program_id(0) % NT == 0)


def _conv_ffn(h, g, w_up, conv_w, conv_b, w_down):
    rows = h.shape[0]
    tile = pl.BlockSpec((TM, D), lambda r: (r, 0))
    halo = pl.BlockSpec((HALO, D), lambda r: (jnp.maximum(r * (TM // HALO) - 1, 0), 0))
    return pl.pallas_call(
        _ffn_kernel,
        grid=(rows // TM,),
        in_specs=[tile, halo, _full((1, D)), _full((D, 2 * FF)), _full((3, 2 * FF)),
                  _full((1, 2 * FF)), _full((FF, D))],
        out_specs=tile,
        out_shape=jax.ShapeDtypeStruct((rows, D), f32),
        compiler_params=pltpu.CompilerParams(
            dimension_semantics=("arbitrary",), vmem_limit_bytes=VMEM_LIMIT),
        name="conv_ffn",
    )(h, h, g, w_up, conv_w, conv_b, w_down)


def _tail_kernel(h_ref, hhalo_ref, a_ref, ahalo_ref, wo_ref, g_ref, wup_ref, cw_ref, cb_ref, wdn_ref,
                 fg_ref, o_ref):
    he = jnp.concatenate([hhalo_ref[...], h_ref[...]], axis=0)
    ae = jnp.concatenate([ahalo_ref[...], a_ref[...]], axis=0)
    xe = he + jnp.dot(ae, wo_ref[...], preferred_element_type=f32)
    y = _ffn_tile(xe, g_ref, wup_ref, cw_ref, cb_ref, wdn_ref, TMF, None)
    o_ref[...] = _rmsnorm(y, fg_ref[...])


def _attn_tail(h, attn, w_o, g, w_up, conv_w, conv_b, w_down, final_g):
    B = h.shape[0] // TP
    nt = SEQ // TMF
    start16 = lambda r: (r // nt) * (TP // HALO) + (r % nt) * (TMF // HALO)
    tile = pl.BlockSpec((pl.Element(TMF), pl.Element(D)), lambda r: (HALO * (start16(r) + 1), 0))
    halo = pl.BlockSpec((pl.Element(HALO), pl.Element(D)), lambda r: (HALO * start16(r), 0))
    return pl.pallas_call(
        _tail_kernel,
        grid=(B * nt,),
        in_specs=[tile, halo, tile, halo, _full((D, D)), _full((1, D)), _full((D, 2 * FF)),
                  _full((3, 2 * FF)), _full((1, 2 * FF)), _full((FF, D)), _full((1, D))],
        out_specs=pl.BlockSpec((None, TMF, D), lambda r: (r // nt, r % nt, 0)),
        out_shape=jax.ShapeDtypeStruct((B, SEQ, D), f32),
        compiler_params=pltpu.CompilerParams(
            dimension_semantics=("arbitrary",), vmem_limit_bytes=VMEM_LIMIT),
        name="attn_tail",
    )(h, h, attn, attn, w_o, g, w_up, conv_w, conv_b, w_down, final_g)


def _cast_kernel(w_ref, o_ref):
    o_ref[...] = w_ref[...].astype(bf16)


def _layer_bf16(w, layer):
    _, rows, cols = w.shape
    return pl.pallas_call(
        _cast_kernel,
        grid=(CAST_STEPS,),
        in_specs=[pl.BlockSpec((None, rows // CAST_STEPS, cols), lambda r: (layer, r, 0))],
        out_specs=pl.BlockSpec((rows // CAST_STEPS, cols), lambda r: (r, 0)),
        out_shape=jax.ShapeDtypeStruct((rows, cols), bf16),
        compiler_params=pltpu.CompilerParams(
            dimension_semantics=("arbitrary",), vmem_limit_bytes=VMEM_LIMIT),
        name="weight_to_bf16",
    )(w)


def _rope_tables():
    inv = 1.0 / (10000.0 ** (jnp.arange(0, HD, 2, dtype=f32) / HD))
    pos = jnp.arange(TP, dtype=f32)
    ang = pos[:, None] * inv[None, :]
    cos = jnp.tile(jnp.cos(ang), (1, 4))
    sin = jnp.tile(jnp.concatenate([-jnp.sin(ang), jnp.sin(ang)], axis=1), (1, 2))
    return cos, sin


def kernel(x, meta_tokens, mix_norm_g, lru_w_in, lru_b_in, lru_conv_w, lru_conv_b, lru_w_a, lru_b_a, lru_w_i, lru_b_i, lru_L, lru_w_out, lru_b_out, attn_w_qkv, attn_lambda_q1, attn_lambda_k1, attn_lambda_q2, attn_lambda_k2, attn_subln_g, attn_w_o, ffn_norm_g, ffn_w_up, ffn_conv_w, ffn_conv_b, ffn_w_down, final_norm_g):
    B = x.shape[0]
    row = lambda v: v.reshape(1, -1)

    w_gate = jnp.concatenate([lru_w_a[0], lru_w_i[0]], axis=-1)
    h = _lru_block(x, meta_tokens.astype(x.dtype), row(mix_norm_g[0]), lru_w_in[0], row(lru_b_in[0]),
                   lru_conv_w[0], row(lru_conv_b[0]), w_gate, row(lru_b_a[0]), row(lru_b_i[0]),
                   row(lru_L[0]), lru_w_out[0], row(lru_b_out[0]))
    h2 = _conv_ffn(h.reshape(B * TP, D), row(ffn_norm_g[0]), _layer_bf16(ffn_w_up, 0), ffn_conv_w[0],
                   row(ffn_conv_b[0]), _layer_bf16(ffn_w_down, 0))

    lambda_init = 0.8 - 0.6 * math.exp(-0.3 * 1)
    cos, sin = _rope_tables()
    qt, k, vt = _qkv_proj(h2.reshape(B, TP, D), row(mix_norm_g[1]), attn_w_qkv[0], cos, sin)
    attn = _diff_attention(qt, k, vt, row(attn_lambda_q1[0]), row(attn_lambda_k1[0]),
                           row(attn_lambda_q2[0]), row(attn_lambda_k2[0]), row(attn_subln_g[0]),
                           lambda_init)
    return _attn_tail(h2, attn.reshape(B * TP, D), attn_w_o[0].astype(bf16), row(ffn_norm_g[1]),
                      _layer_bf16(ffn_w_up, 1), ffn_conv_w[1], row(ffn_conv_b[1]),
                      _layer_bf16(ffn_w_down, 1), row(final_norm_g))
```

```python
import functools
import math

import jax
import jax.numpy as jnp
from jax import lax
from jax.experimental import pallas as pl
from jax.experimental.pallas import tpu as pltpu

f32 = jnp.float32
bf16 = jnp.bfloat16

D = 1024
FF = 3072
NH = 8
HW = 128
HD = 64
SEQ = 8192
N_META = 16
PAD = 112
TP = N_META + SEQ + PAD
TM = 640
NT = TP // TM
TQ = 640
TK = 640
QG = 128
NG = TQ // QG
UNROLL = 4
LOG2E = math.log2(math.e)
TMF = 512
HALO = 16
FC = 1024
CAST_STEPS = 4
EPS = 1e-6
NEG = -1e30
VMEM_LIMIT = 56 * 1024 * 1024


def _rmsnorm(x, g):
    ms = jnp.mean(x * x, axis=-1, keepdims=True)
    return x * lax.rsqrt(ms + EPS) * g


def _gelu_tanh(x):
    k1 = -2.0 * math.sqrt(2.0 / math.pi) * LOG2E
    return x * (1.0 / (1.0 + jnp.exp2(x * (k1 + (k1 * 0.044715) * (x * x)))))


def _full(shape):
    n = len(shape)
    return pl.BlockSpec(shape, lambda *_: (0,) * n)


def _lru_kernel(x_ref, xn_ref, meta_ref, g_ref, win_ref, bin_ref, cw_ref, cb_ref, wg_ref, ba_ref, bi_ref,
                lam_ref, wout_ref, bout_ref, o_ref, ua_scr, ub_scr, xbuf, hc):
    c = pl.program_id(1)

    def load_tile(ref, cc):
        first = jnp.concatenate([meta_ref[...], ref[0:TM - N_META, :]], axis=0)
        body = jnp.where(cc == 0, first[0:TM - PAD], ref[0:TM - PAD, :])
        tail = jnp.where(cc == 0, first[TM - PAD:TM], jnp.where(cc == NT - 1, 0.0, ref[TM - PAD:TM, :]))
        return jnp.concatenate([body, tail], axis=0)

    def in_proj(xt, dst):
        hn = _rmsnorm(xt, g_ref[...])
        dst[...] = jnp.dot(hn, win_ref[...], preferred_element_type=f32) + bin_ref[...]

    @pl.when(c == 0)
    def _():
        in_proj(load_tile(x_ref, c), ua_scr)
        xbuf[0:8, :] = jnp.zeros((8, D), f32)
        hc[...] = jnp.zeros((8, D), f32)

    def stage(cur, nxt):
        rec = cur[:, D:]
        xbuf[8:8 + TM, :] = rec
        cw = cw_ref[...]
        conv = (cb_ref[...] + cw[3:4] * rec + cw[2:3] * xbuf[7:7 + TM, :]
                + cw[1:2] * xbuf[6:6 + TM, :] + cw[0:1] * xbuf[5:5 + TM, :])
        xbuf[0:8, :] = xbuf[TM:TM + 8, :]

        pa, pi = [], []
        for hd in range(NH):
            ga = jnp.dot(conv[:, hd * HW:(hd + 1) * HW], wg_ref[hd], preferred_element_type=f32)
            pa.append(ga[:, :HW])
            pi.append(ga[:, HW:])

        in_proj(load_tile(xn_ref, jnp.minimum(c + 1, NT - 1)), nxt)

        r = jax.nn.sigmoid(jnp.concatenate(pa, axis=1) + ba_ref[...])
        gi = jax.nn.sigmoid(jnp.concatenate(pi, axis=1) + bi_ref[...])
        lam = lam_ref[...]
        log_sig = jnp.minimum(lam, 0.0) - jnp.log1p(jnp.exp(-jnp.abs(lam)))
        a = jnp.exp2(r * (8.0 * LOG2E * log_sig))
        mult = jnp.sqrt(1.0 - a * a)
        u = mult * (gi * conv)

        sub = lax.broadcasted_iota(jnp.int32, (8, D), 0)
        carry = hc[...]
        hs = []
        for i in range(TM // 8):
            av = a[i * 8:(i + 1) * 8]
            hv = u[i * 8:(i + 1) * 8]
            for s in (1, 2, 4):
                keep = sub >= s
                a_sh = jnp.where(keep, pltpu.roll(av, s, 0), 1.0)
                h_sh = jnp.where(keep, pltpu.roll(hv, s, 0), 0.0)
                hv = av * h_sh + hv
                av = av * a_sh
            hv = hv + av * carry
            hs.append(hv)
            carry = jnp.broadcast_to(hv[7:8, :], (8, D))
        hc[...] = carry
        hseq = jnp.concatenate(hs, axis=0)

        y = _gelu_tanh(cur[:, :D]) * hseq
        o_ref[0] = (load_tile(x_ref, c) + jnp.dot(y, wout_ref[...], preferred_element_type=f32)
                    + bout_ref[...])

    @pl.when(c % 2 == 0)
    def _():
        stage(ua_scr, ub_scr)

    @pl.when(c % 2 == 1)
    def _():
        stage(ub_scr, ua_scr)


def _lru_block(x, meta, g, w_in, b_in, conv_w, conv_b, w_gate, b_a, b_i, lam, w_out, b_out):
    B = x.shape[0]
    tile = pl.BlockSpec((1, TM, D), lambda b, c: (b, c, 0))
    start = lambda c: 8 * jnp.maximum(c * (TM // 8) - N_META // 8, 0)
    window = pl.BlockSpec((None, pl.Element(TM, (0, PAD)), pl.Element(D)), lambda b, c: (b, start(c), 0))
    window_next = pl.BlockSpec((None, pl.Element(TM, (0, PAD)), pl.Element(D)),
                               lambda b, c: (b, start(jnp.minimum(c + 1, NT - 1)), 0))
    return pl.pallas_call(
        _lru_kernel,
        grid=(B, NT),
        in_specs=[window, window_next, _full((N_META, D)), _full((1, D)), _full((D, 2 * D)), _full((1, 2 * D)), _full((4, D)),
                  _full((1, D)), _full((NH, HW, 2 * HW)), _full((1, D)), _full((1, D)),
                  _full((1, D)), _full((D, D)), _full((1, D))],
        out_specs=tile,
        out_shape=jax.ShapeDtypeStruct((B, TP, D), f32),
        scratch_shapes=[pltpu.VMEM((TM, 2 * D), f32), pltpu.VMEM((TM, 2 * D), f32),
                        pltpu.VMEM((TM + 8, D), f32), pltpu.VMEM((8, D), f32)],
        compiler_params=pltpu.CompilerParams(
            dimension_semantics=("arbitrary", "arbitrary"), vmem_limit_bytes=VMEM_LIMIT),
        name="lru_block",
    )(x, x, meta, g, w_in, b_in, conv_w, conv_b, w_gate, b_a, b_i, lam, w_out, b_out)


def _qkv_kernel(h_ref, g_ref, w_ref, cos_ref, sin_ref, qt_ref, k_ref, vt_ref):
    hn = _rmsnorm(h_ref[0], g_ref[...])
    qkv = jnp.dot(hn, w_ref[...], preferred_element_type=f32)
    cos = cos_ref[...]
    sin = sin_ref[...]
    lane = lax.broadcasted_iota(jnp.int32, (TM, HW), 1)
    first = (lane & (HD - 1)) < HD // 2

    def rope(t):
        rot = jnp.where(first, pltpu.roll(t, HW - HD // 2, 1), pltpu.roll(t, HD // 2, 1))
        return t * cos + rot * sin

    for hd in range(NH):
        qh = rope(qkv[:, hd * HW:(hd + 1) * HW]) * (HD ** -0.5 * LOG2E)
        qt_ref[0, hd, 0] = qh.T.astype(bf16)
        kh = rope(qkv[:, D + hd * HW:D + (hd + 1) * HW])
        k_ref[0, :, hd * HW:(hd + 1) * HW] = kh.astype(bf16)
        vh = qkv[:, 2 * D + hd * HW:2 * D + (hd + 1) * HW]
        vt_ref[0, hd, 0] = vh.T.astype(bf16)


def _qkv_proj(h, g, w_qkv, cos, sin):
    B = h.shape[0]
    return pl.pallas_call(
        _qkv_kernel,
        grid=(B, NT),
        in_specs=[pl.BlockSpec((1, TM, D), lambda b, c: (b, c, 0)), _full((1, D)),
                  _full((D, 3 * D)),
                  pl.BlockSpec((TM, HW), lambda b, c: (c, 0)),
                  pl.BlockSpec((TM, HW), lambda b, c: (c, 0))],
        out_specs=[pl.BlockSpec((1, NH, 1, HW, TM), lambda b, c: (b, 0, c, 0, 0)),
                   pl.BlockSpec((1, TM, D), lambda b, c: (b, c, 0)),
                   pl.BlockSpec((1, NH, 1, HW, TM), lambda b, c: (b, 0, c, 0, 0))],
        out_shape=[jax.ShapeDtypeStruct((B, NH, NT, HW, TM), bf16),
                   jax.ShapeDtypeStruct((B, TP, D), bf16),
                   jax.ShapeDtypeStruct((B, NH, NT, HW, TM), bf16)],
        compiler_params=pltpu.CompilerParams(
            dimension_semantics=("arbitrary", "arbitrary"), vmem_limit_bytes=VMEM_LIMIT),
        name="qkv_proj",
    )(h, g, w_qkv, cos, sin)


def _attn_kernel(qt_ref, k_ref, vt_ref, lq1_ref, lk1_ref, lq2_ref, lk2_ref, sg_ref, o_ref,
                 q2_scr, sa_scr, sb_scr, m_scr, l_scr, acc_scr, *, lambda_init):
    lam = (jnp.exp(jnp.sum(lq1_ref[...] * lk1_ref[...], axis=1, keepdims=True))
           - jnp.exp(jnp.sum(lq2_ref[...] * lk2_ref[...], axis=1, keepdims=True)) + lambda_init)

    def scores(j, s_ref):
        kk = k_ref[0, pl.ds(pl.multiple_of(j * TK, TK), TK), :]
        for n in range(NG):
            s_ref[n] = jnp.dot(kk, q2_scr[:, 2 * n * QG:(2 * n + 2) * QG], preferred_element_type=f32)

    def update(n, j, s_ref, diag):
        cols = slice(2 * n * QG, (2 * n + 2) * QG)
        if diag:
            rows = (n + 1) * QG
            r = lax.broadcasted_iota(jnp.int32, (QG, 2 * QG), 0)
            c = lax.broadcasted_iota(jnp.int32, (QG, 2 * QG), 1) & (QG - 1)
            bot = jnp.where(r <= c, s_ref[n, rows - QG:rows, :], NEG)
            s = bot if rows == QG else jnp.concatenate([s_ref[n, :rows - QG, :], bot], axis=0)
        else:
            rows = TK
            s = s_ref[n]
        m_prev = m_scr[:, cols]
        m_new = jnp.maximum(m_prev, jnp.max(s, axis=0, keepdims=True))
        alpha = jnp.exp2(m_prev - m_new)
        p = jnp.exp2(s - m_new)
        l_scr[:, cols] = alpha * l_scr[:, cols] + jnp.sum(p, axis=0, keepdims=True)
        acc_scr[:, cols] = alpha * acc_scr[:, cols] + jnp.dot(
            vt_ref[0, 0, j, :, :rows], p.astype(bf16), preferred_element_type=f32)
        m_scr[:, cols] = m_new

    bufs = (sa_scr, sb_scr)

    def full_steps(j0, count):
        for t in range(count):
            scores(j0 + t + 1, bufs[(t + 1) % 2])
            for n in range(NG):
                update(n, j0 + t, bufs[t % 2], False)

    def query_tile(i, carry):
        feat = lax.broadcasted_iota(jnp.int32, (HW, QG), 0)
        zero = jnp.zeros((HW, QG), bf16)
        for n in range(NG):
            qg = qt_ref[0, 0, i, :, n * QG:(n + 1) * QG]
            q2_scr[:, 2 * n * QG:(2 * n + 1) * QG] = jnp.where(feat < HD, qg, zero)
            q2_scr[:, (2 * n + 1) * QG:(2 * n + 2) * QG] = jnp.where(feat >= HD, qg, zero)
        m_scr[...] = jnp.full((1, 2 * TQ), NEG, f32)
        l_scr[...] = jnp.zeros((1, 2 * TQ), f32)
        acc_scr[...] = jnp.zeros((HW, 2 * TQ), f32)

        scores(0, sa_scr)

        def quad(q, c):
            full_steps(UNROLL * q, UNROLL)
            return c

        lax.fori_loop(0, i // UNROLL, quad, 0)

        for rem in range(UNROLL):
            @pl.when(i % UNROLL == rem)
            def _(rem=rem):
                full_steps(i - rem, rem)
                for n in range(NG):
                    update(n, i, bufs[rem % 2], True)

        for n in range(NG):
            c1 = slice(2 * n * QG, (2 * n + 1) * QG)
            c2 = slice((2 * n + 1) * QG, (2 * n + 2) * QG)
            o = acc_scr[:, c1] * (1.0 / l_scr[:, c1]) - lam * (acc_scr[:, c2] * (1.0 / l_scr[:, c2]))
            y = _rmsnorm(o.T, sg_ref[...]) * (1.0 - lambda_init)
            o_ref[0, pl.ds(pl.multiple_of(i * TQ + n * QG, QG), QG), :] = y.astype(bf16)
        return carry

    lax.fori_loop(0, TP // TQ, query_tile, 0)


def _diff_attention(qt, k, vt, lq1, lk1, lq2, lk2, subln_g, lambda_init):
    B = k.shape[0]
    return pl.pallas_call(
        functools.partial(_attn_kernel, lambda_init=lambda_init),
        grid=(B, NH),
        in_specs=[pl.BlockSpec((1, 1, NT, HW, TM), lambda b, h: (b, h, 0, 0, 0)),
                  pl.BlockSpec((1, TP, HW), lambda b, h: (b, 0, h)),
                  pl.BlockSpec((1, 1, NT, HW, TM), lambda b, h: (b, h, 0, 0, 0)),
                  _full((1, HD)), _full((1, HD)), _full((1, HD)), _full((1, HD)),
                  _full((1, HW))],
        out_specs=pl.BlockSpec((1, TP, HW), lambda b, h: (b, 0, h)),
        out_shape=jax.ShapeDtypeStruct((B, TP, D), bf16),
        scratch_shapes=[pltpu.VMEM((HW, 2 * TQ), bf16), pltpu.VMEM((NG, TK, 2 * QG), f32),
                        pltpu.VMEM((NG, TK, 2 * QG), f32), pltpu.VMEM((1, 2 * TQ), f32),
                        pltpu.VMEM((1, 2 * TQ), f32), pltpu.VMEM((HW, 2 * TQ), f32)],
        compiler_params=pltpu.CompilerParams(
            dimension_semantics=("arbitrary", "arbitrary"), vmem_limit_bytes=VMEM_LIMIT),
        name="diff_attention",
    )(qt, k, vt, lq1, lk1, lq2, lk2, subln_g)


def _ffn_tile(xe, g_ref, wup_ref, cw_ref, cb_ref, wdn_ref, tm, blank_halo):
    hn = _rmsnorm(xe, g_ref[...])
    if blank_halo is not None:
        row = lax.broadcasted_iota(jnp.int32, (HALO + tm, 1), 0)
        hn = jnp.where(jnp.logical_or(row >= HALO, jnp.logical_not(blank_halo)), hn, 0.0)
    hn = hn.astype(bf16)

    def conv(u, lo):
        w = cw_ref[:, lo:lo + FC]
        return (cb_ref[:, lo:lo + FC] + w[2:3] * u[HALO:HALO + tm] + w[1:2] * u[HALO - 1:HALO - 1 + tm]
                + w[0:1] * u[HALO - 2:HALO - 2 + tm])

    def up(ch):
        lo = ch * FC
        return (jnp.dot(hn, wup_ref[:, lo:lo + FC], preferred_element_type=f32),
                jnp.dot(hn, wup_ref[:, FF + lo:FF + lo + FC], preferred_element_type=f32))

    acc = xe[HALO:]
    nxt = up(0)
    for ch in range(FF // FC):
        lo = ch * FC
        ug, uv = nxt
        if ch + 1 < FF // FC:
            nxt = up(ch + 1)
        act = (_gelu_tanh(conv(ug, lo)) * conv(uv, FF + lo)).astype(bf16)
        acc = acc + jnp.dot(act, wdn_ref[lo:lo + FC, :], preferred_element_type=f32)
    return acc


def _ffn_kernel(h_ref, halo_ref, g_ref, wup_ref, cw_ref, cb_ref, wdn_ref, o_ref):
    xe = jnp.concatenate([halo_ref[...], h_ref[...]], axis=0)
    o_ref[...] = _ffn_tile(xe, g_ref, wup_ref, cw_ref, cb_ref, wdn_ref, TM, pl.program_id(0) % NT == 0)


def _conv_ffn(h, g, w_up, conv_w, conv_b, w_down):
    rows = h.shape[0]
    tile = pl.BlockSpec((TM, D), lambda r: (r, 0))
    halo = pl.BlockSpec((HALO, D), lambda r: (jnp.maximum(r * (TM // HALO) - 1, 0), 0))
    return pl.pallas_call(
        _ffn_kernel,
        grid=(rows // TM,),
        in_specs=[tile, halo, _full((1, D)), _full((D, 2 * FF)), _full((3, 2 * FF)),
                  _full((1, 2 * FF)), _full((FF, D))],
        out_specs=tile,
        out_shape=jax.ShapeDtypeStruct((rows, D), f32),
        compiler_params=pltpu.CompilerParams(
            dimension_semantics=("arbitrary",), vmem_limit_bytes=VMEM_LIMIT),
        name="conv_ffn",
    )(h, h, g, w_up, conv_w, conv_b, w_down)


def _tail_kernel(h_ref, hhalo_ref, a_ref, ahalo_ref, wo_ref, g_ref, wup_ref, cw_ref, cb_ref, wdn_ref,
                 fg_ref, o_ref):
    he = jnp.concatenate([hhalo_ref[...], h_ref[...]], axis=0)
    ae = jnp.concatenate([ahalo_ref[...], a_ref[...]], axis=0)
    xe = he + jnp.dot(ae, wo_ref[...], preferred_element_type=f32)
    y = _ffn_tile(xe, g_ref, wup_ref, cw_ref, cb_ref, wdn_ref, TMF, None)
    o_ref[...] = _rmsnorm(y, fg_ref[...])


def _attn_tail(h, attn, w_o, g, w_up, conv_w, conv_b, w_down, final_g):
    B = h.shape[0] // TP
    nt = SEQ // TMF
    start16 = lambda r: (r // nt) * (TP // HALO) + (r % nt) * (TMF // HALO)
    tile = pl.BlockSpec((pl.Element(TMF), pl.Element(D)), lambda r: (HALO * (start16(r) + 1), 0))
    halo = pl.BlockSpec((pl.Element(HALO), pl.Element(D)), lambda r: (HALO * start16(r), 0))
    return pl.pallas_call(
        _tail_kernel,
        grid=(B * nt,),
        in_specs=[tile, halo, tile, halo, _full((D, D)), _full((1, D)), _full((D, 2 * FF)),
                  _full((3, 2 * FF)), _full((1, 2 * FF)), _full((FF, D)), _full((1, D))],
        out_specs=pl.BlockSpec((None, TMF, D), lambda r: (r // nt, r % nt, 0)),
        out_shape=jax.ShapeDtypeStruct((B, SEQ, D), f32),
        compiler_params=pltpu.CompilerParams(
            dimension_semantics=("arbitrary",), vmem_limit_bytes=VMEM_LIMIT),
        name="attn_tail",
    )(h, h, attn, attn, w_o, g, w_up, conv_w, conv_b, w_down, final_g)


def _cast_kernel(w_ref, o_ref):
    o_ref[...] = w_ref[...].astype(bf16)


def _layer_bf16(w, layer):
    _, rows, cols = w.shape
    return pl.pallas_call(
        _cast_kernel,
        grid=(CAST_STEPS,),
        in_specs=[pl.BlockSpec((None, rows // CAST_STEPS, cols), lambda r: (layer, r, 0))],
        out_specs=pl.BlockSpec((rows // CAST_STEPS, cols), lambda r: (r, 0)),
        out_shape=jax.ShapeDtypeStruct((rows, cols), bf16),
        compiler_params=pltpu.CompilerParams(
            dimension_semantics=("arbitrary",), vmem_limit_bytes=VMEM_LIMIT),
        name="weight_to_bf16",
    )(w)


def _rope_tables():
    inv = 1.0 / (10000.0 ** (jnp.arange(0, HD, 2, dtype=f32) / HD))
    pos = jnp.arange(TP, dtype=f32)
    ang = pos[:, None] * inv[None, :]
    cos = jnp.tile(jnp.cos(ang), (1, 4))
    sin = jnp.tile(jnp.concatenate([-jnp.sin(ang), jnp.sin(ang)], axis=1), (1, 2))
    return cos, sin


def kernel(x, meta_tokens, mix_norm_g, lru_w_in, lru_b_in, lru_conv_w, lru_conv_b, lru_w_a, lru_b_a, lru_w_i, lru_b_i, lru_L, lru_w_out, lru_b_out, attn_w_qkv, attn_lambda_q1, attn_lambda_k1, attn_lambda_q2, attn_lambda_k2, attn_subln_g, attn_w_o, ffn_norm_g, ffn_w_up, ffn_conv_w, ffn_conv_b, ffn_w_down, final_norm_g):
    B = x.shape[0]
    row = lambda v: v.reshape(1, -1)

    w_gate = jnp.concatenate([lru_w_a[0], lru_w_i[0]], axis=-1)
    h = _lru_block(x, meta_tokens.astype(x.dtype), row(mix_norm_g[0]), lru_w_in[0], row(lru_b_in[0]),
                   lru_conv_w[0], row(lru_conv_b[0]), w_gate, row(lru_b_a[0]), row(lru_b_i[0]),
                   row(lru_L[0]), lru_w_out[0], row(lru_b_out[0]))
    h2 = _conv_ffn(h.reshape(B * TP, D), row(ffn_norm_g[0]), _layer_bf16(ffn_w_up, 0), ffn_conv_w[0],
                   row(ffn_conv_b[0]), _layer_bf16(ffn_w_down, 0))

    lambda_init = 0.8 - 0.6 * math.exp(-0.3 * 1)
    cos, sin = _rope_tables()
    qt, k, vt = _qkv_proj(h2.reshape(B, TP, D), row(mix_norm_g[1]), attn_w_qkv[0], cos, sin)
    attn = _diff_attention(qt, k, vt, row(attn_lambda_q1[0]), row(attn_lambda_k1[0]),
                           row(attn_lambda_q2[0]), row(attn_lambda_k2[0]), row(attn_subln_g[0]),
                           lambda_init)
    return _attn_tail(h2, attn.reshape(B * TP, D), attn_w_o[0].astype(bf16), row(ffn_norm_g[1]),
                      _layer_bf16(ffn_w_up, 1), ffn_conv_w[1], row(ffn_conv_b[1]),
                      _layer_bf16(ffn_w_down, 1), row(final_norm_g))
```
